```python
import jax, jax.numpy as jnp
from jax import lax
import numpy as np

D_MODEL = 2048
BATCH = 2
SEQ = 8192
DEPTH = 1

SB_HEADS = 16
SB_HEAD_DIM = 128
SB_WIDTH = SB_HEADS * SB_HEAD_DIM
SB_BLOCK = 128
GDN_HEADS = 16
GDN_KEY_DIM = 128
GDN_VAL_DIM = 128
GDN_QK_WIDTH = GDN_HEADS * GDN_KEY_DIM
GDN_V_WIDTH = GDN_HEADS * GDN_VAL_DIM
GDN_CONV = 4
GDN_CONV_DIM = 2 * GDN_QK_WIDTH + GDN_V_WIDTH
GDN_CHUNK = 64
IN_WIDTH = 4 * SB_WIDTH + 2 * GDN_QK_WIDTH + 2 * GDN_V_WIDTH + 2 * GDN_HEADS + 2 * D_MODEL
NORM_EPS = 1e-6
L2_EPS = 1e-6

kernel_name = "hybrid_stickbreaking_gated_deltanet_adaln"


def _in_split_points():
    sizes = [SB_WIDTH] * 4 + [GDN_QK_WIDTH, GDN_QK_WIDTH, GDN_V_WIDTH, GDN_V_WIDTH,
                              GDN_HEADS, GDN_HEADS, D_MODEL, D_MODEL]
    return [int(p) for p in np.cumsum(sizes)[:-1]]


def _rmsnorm(x, w):
    xf = x.astype(jnp.float32)
    y = xf * lax.rsqrt(jnp.mean(xf * xf, axis=-1, keepdims=True) + NORM_EPS)
    return (y * w.astype(jnp.float32)).astype(x.dtype)


def _l2norm(x):
    return x * lax.rsqrt(jnp.sum(x * x, axis=-1, keepdims=True) + L2_EPS)


def _heads(x, n_heads):
    b, t, _ = x.shape
    return x.reshape(b, t, n_heads, -1).transpose(0, 2, 1, 3)


def _merge_heads(x):
    b, h, t, d = x.shape
    return x.transpose(0, 2, 1, 3).reshape(b, t, h * d)


def _causal_short_conv(x, w):
    k, ch = w.shape
    return lax.conv_general_dilated(
        x, w[:, None, :].astype(x.dtype), window_strides=(1,), padding=[(k - 1, 0)],
        dimension_numbers=("NWC", "WIO", "NWC"), feature_group_count=ch)


def _stick_breaking_attention(q, k, v):
    t_len = q.shape[2]
    scale = SB_HEAD_DIM ** -0.5
    outs = []
    for blk in range(t_len // SB_BLOCK):
        t0 = blk * SB_BLOCK
        t1 = t0 + SB_BLOCK
        qb = q[:, :, t0:t1].astype(jnp.float32)
        kb = k[:, :, :t1].astype(jnp.float32)
        vb = v[:, :, :t1].astype(jnp.float32)
        z = jnp.einsum("bhtd,bhsd->bhts", qb, kb) * scale
        t_idx = t0 + jnp.arange(SB_BLOCK)[:, None]
        s_idx = jnp.arange(t1)[None, :]
        mask = s_idx < t_idx
        log_beta = jax.nn.log_sigmoid(z)
        log_not_beta = jnp.where(mask, jax.nn.log_sigmoid(-z), 0.0)
        later = lax.cumsum(log_not_beta, axis=3, reverse=True) - log_not_beta
        att = jnp.where(mask, jnp.exp(log_beta + later), 0.0)
        outs.append(jnp.einsum("bhts,bhsd->bhtd", att, vb))
    return jnp.concatenate(outs, axis=2)


def _gated_delta_rule(q, k, v, beta, g):
    b, h, t_len, dk = q.shape
    dv = v.shape[-1]
    c = GDN_CHUNK
    n = t_len // c
    q = q * dk ** -0.5

    def chunks(a):
        return a.reshape(b, h, n, c, *a.shape[3:])

    q, k, v, beta, g = chunks(q), chunks(k), chunks(v), chunks(beta), chunks(g)
    g = jnp.cumsum(g, axis=-1)
    k_beta = k * beta[..., None]
    v_beta = v * beta[..., None]
    tril = jnp.tril(jnp.ones((c, c), dtype=bool))
    strict = jnp.tril(jnp.ones((c, c), dtype=bool), -1)
    decay = jnp.exp(jnp.where(tril, g[..., :, None] - g[..., None, :], -jnp.inf))
    lower = jnp.where(strict, jnp.einsum("bhncd,bhnsd->bhncs", k_beta, k) * decay, 0.0)
    eye = jnp.eye(c, dtype=jnp.float32)
    t_mat = lax.linalg.triangular_solve(eye + lower, jnp.broadcast_to(eye, lower.shape),
                                        left_side=True, lower=True, unit_diagonal=True)
    u = jnp.einsum("bhncs,bhnsv->bhncv", t_mat, v_beta)
    w = jnp.einsum("bhncs,bhnsk->bhnck", t_mat, k_beta * jnp.exp(g)[..., None])
    intra = jnp.where(tril, jnp.einsum("bhncd,bhnsd->bhncs", q, k) * decay, 0.0)
    q_dec = q * jnp.exp(g)[..., None]
    k_dec = k * jnp.exp(g[..., -1:] - g)[..., None]
    g_last = jnp.exp(g[..., -1])

    def step(state, inp):
        q_i, k_i, u_i, w_i, intra_i, gl_i = inp
        v_new = u_i - jnp.einsum("bhck,bhkv->bhcv", w_i, state)
        o_i = (jnp.einsum("bhck,bhkv->bhcv", q_i, state)
               + jnp.einsum("bhcs,bhsv->bhcv", intra_i, v_new))
        state = state * gl_i[..., None, None] + jnp.einsum("bhck,bhcv->bhkv", k_i, v_new)
        return state, o_i

    xs = tuple(jnp.moveaxis(a, 2, 0) for a in (q_dec, k_dec, u, w, intra, g_last))
    s0 = jnp.zeros((b, h, dk, dv), dtype=jnp.float32)
    _, o = lax.scan(step, s0, xs)
    return jnp.moveaxis(o, 0, 2).reshape(b, h, t_len, dv)


def setup_inputs(seed: int = 0) -> dict:
    key = jax.random.key(seed)
    ks = jax.random.split(key, 16)
    f32 = jnp.float32
    x = jax.random.normal(ks[0], (BATCH, SEQ, D_MODEL), f32)
    c = jax.random.normal(ks[1], (BATCH, D_MODEL), f32)
    w_ada = jax.random.normal(ks[2], (DEPTH, D_MODEL, 3 * D_MODEL), f32) * D_MODEL ** -0.5
    b_ada = jax.random.normal(ks[3], (DEPTH, 3 * D_MODEL), f32) * 0.02
    norm_w = 1.0 + 0.02 * jax.random.normal(ks[4], (DEPTH, D_MODEL), f32)
    w_in = jax.random.normal(ks[5], (DEPTH, D_MODEL, IN_WIDTH), f32) * D_MODEL ** -0.5
    gdn_conv_w = jax.random.normal(ks[6], (DEPTH, GDN_CONV, GDN_CONV_DIM), f32) * GDN_CONV ** -0.5
    gdn_a_log = jnp.log(jax.random.uniform(ks[7], (DEPTH, GDN_HEADS), f32, 1.0, 16.0))
    dt = jnp.exp(jax.random.uniform(ks[8], (DEPTH, GDN_HEADS), f32, jnp.log(1e-3), jnp.log(1e-1)))
    gdn_dt_bias = dt + jnp.log(-jnp.expm1(-dt))
    gdn_norm_w = 1.0 + 0.02 * jax.random.normal(ks[9], (DEPTH, GDN_VAL_DIM), f32)
    w_proj_sb = jax.random.normal(ks[10], (DEPTH, SB_WIDTH, D_MODEL), f32) * SB_WIDTH ** -0.5
    w_proj_gdn = jax.random.normal(ks[11], (DEPTH, GDN_V_WIDTH, D_MODEL), f32) * GDN_V_WIDTH ** -0.5
    w_out = jax.random.normal(ks[12], (DEPTH, D_MODEL, D_MODEL), f32) * D_MODEL ** -0.5
    final_norm_w = 1.0 + 0.02 * jax.random.normal(ks[13], (D_MODEL,), f32)
    return {"x": x, "c": c, "w_ada": w_ada, "b_ada": b_ada, "norm_w": norm_w, "w_in": w_in,
            "gdn_conv_w": gdn_conv_w, "gdn_a_log": gdn_a_log, "gdn_dt_bias": gdn_dt_bias,
            "gdn_norm_w": gdn_norm_w, "w_proj_sb": w_proj_sb, "w_proj_gdn": w_proj_gdn,
            "w_out": w_out, "final_norm_w": final_norm_w}


def reference(x, c, w_ada, b_ada, norm_w, w_in, gdn_conv_w, gdn_a_log, gdn_dt_bias,
              gdn_norm_w, w_proj_sb, w_proj_gdn, w_out, final_norm_w):
    f32 = jnp.float32
    split_points = _in_split_points()
    for l in range(DEPTH):
        mod = jax.nn.silu(c) @ w_ada[l] + b_ada[l]
        shift, scale, gate = jnp.split(mod, 3, axis=-1)
        h = _rmsnorm(x, norm_w[l]) * (1.0 + scale[:, None, :]) + shift[:, None, :]
        proj = h @ w_in[l]
        (sb_q, sb_k, sb_v, sb_z, gq, gk, gv, gz, gb, ga, m_sb, m_gdn) = jnp.split(
            proj, split_points, axis=-1)

        o_sb = _stick_breaking_attention(_heads(sb_q, SB_HEADS), _heads(sb_k, SB_HEADS),
                                         _heads(sb_v, SB_HEADS))
        o_sb = _merge_heads(o_sb).astype(x.dtype) * jax.nn.silu(sb_z)

        qkv = jax.nn.silu(_causal_short_conv(jnp.concatenate([gq, gk, gv], axis=-1), gdn_conv_w[l]))
        cq, ck, cv = jnp.split(qkv, [GDN_QK_WIDTH, 2 * GDN_QK_WIDTH], axis=-1)
        qh = _l2norm(_heads(cq, GDN_HEADS).astype(f32))
        kh = _l2norm(_heads(ck, GDN_HEADS).astype(f32))
        vh = _heads(cv, GDN_HEADS).astype(f32)
        beta = jax.nn.sigmoid(gb.astype(f32)).transpose(0, 2, 1)
        g = (-jnp.exp(gdn_a_log[l].astype(f32))
             * jax.nn.softplus(ga.astype(f32) + gdn_dt_bias[l].astype(f32))).transpose(0, 2, 1)
        o_gdn = _gated_delta_rule(qh, kh, vh, beta, g)
        o_gdn = _rmsnorm(o_gdn, gdn_norm_w[l])
        o_gdn = _merge_heads(o_gdn).astype(x.dtype) * jax.nn.silu(gz)

        y = (jax.nn.sigmoid(m_sb) * (o_sb @ w_proj_sb[l])
             + jax.nn.sigmoid(m_gdn) * (o_gdn @ w_proj_gdn[l]))
        x = x + gate[:, None, :] * (y @ w_out[l])
    return _rmsnorm(x, final_norm_w)
```

```python
import functools

import jax
import jax.numpy as jnp
from jax import lax
from jax.experimental import pallas as pl
from jax.experimental.pallas import tpu as pltpu

F32 = jnp.float32
BF16 = jnp.bfloat16

HEAD_DIM = 128
GDN_CHUNK = 64
GDN_CONV = 4
NORM_EPS = 1e-6
L2_EPS = 1e-6
V7X_VMEM_LIMIT = 56 * 1024 * 1024
SB_DEAD_LOG = 104.0


def _tile(n, pref):
    t = min(n, pref)
    while n % t:
        t //= 2
    return t


def _cparams(sem):
    return pltpu.CompilerParams(dimension_semantics=sem, vmem_limit_bytes=V7X_VMEM_LIMIT)


def _softplus(x):
    return jnp.maximum(x, 0.0) + jnp.log(1.0 + jnp.exp(-jnp.abs(x)))


def _sigmoid(x):
    return 1.0 / (1.0 + jnp.exp(-x))


def _silu(x):
    return x * _sigmoid(x)


def _dot(a, b):
    return jnp.dot(a, b, preferred_element_type=F32)


def _dot_nt(a, b):
    return lax.dot_general(a, b, (((1,), (1,)), ((), ())), preferred_element_type=F32)


def _split3(a):
    hi = a.astype(BF16)
    r1 = a - hi.astype(F32)
    mid = r1.astype(BF16)
    lo = (r1 - mid.astype(F32)).astype(BF16)
    return hi, mid, lo


def _dot_exact_rhs01(a, m01):
    hi, mid, lo = _split3(a)
    return _dot(hi, m01) + _dot(mid, m01) + _dot(lo, m01)


def _dot_exact_lhs01(m01, a):
    hi, mid, lo = _split3(a)
    return _dot(m01, hi) + _dot(m01, mid) + _dot(m01, lo)


def _mod_kernel(c_ref, w_ref, b_ref, o_ref):
    a_hi, a_lo, _ = _split3(_silu(c_ref[...]))
    w_hi, w_lo, _ = _split3(w_ref[...])
    o_ref[...] = _dot(a_hi, w_hi) + _dot(a_hi, w_lo) + _dot(a_lo, w_hi) + b_ref[...]


def _mod_call(c_pad, w_ada, b_ada):
    rows, d = c_pad.shape
    n = w_ada.shape[1]
    tn = _tile(n, 768)
    return pl.pallas_call(
        _mod_kernel,
        grid=(n // tn,),
        in_specs=[pl.BlockSpec((rows, d), lambda j: (0, 0)),
                  pl.BlockSpec((d, tn), lambda j: (0, j)),
                  pl.BlockSpec((1, tn), lambda j: (0, j))],
        out_specs=pl.BlockSpec((rows, tn), lambda j: (0, j)),
        out_shape=jax.ShapeDtypeStruct((rows, n), F32),
        compiler_params=_cparams(("arbitrary",)),
        name="adaln_mod",
    )(c_pad, w_ada, b_ada)


def _modulated_norm(x, nw, scale, shift):
    ms = jnp.mean(x * x, axis=-1, keepdims=True)
    y = x * lax.rsqrt(ms + NORM_EPS) * nw
    return y * (1.0 + scale) + shift


def _inproj_kernel(x_ref, nw_ref, scale_ref, shift_ref, w_ref, o_ref, h_ref):
    @pl.when(pl.program_id(1) == 0)
    def _():
        h = _modulated_norm(x_ref[...], nw_ref[...], scale_ref[0], shift_ref[0])
        h_ref[...] = h.astype(BF16)

    o_ref[...] = _dot(h_ref[...], w_ref[...]).astype(o_ref.dtype)


def _inproj_call(x2, nw, scale3, shift3, w_bf, seq):
    m, d = x2.shape
    n = w_bf.shape[1]
    tm = _tile(seq, 1024)
    tn = _tile(n, 1024)
    per_b = seq // tm
    return pl.pallas_call(
        _inproj_kernel,
        grid=(m // tm, n // tn),
        in_specs=[pl.BlockSpec((tm, d), lambda i, j: (i, 0)),
                  pl.BlockSpec((1, d), lambda i, j: (0, 0)),
                  pl.BlockSpec((1, 1, d), lambda i, j: (i // per_b, 0, 0)),
                  pl.BlockSpec((1, 1, d), lambda i, j: (i // per_b, 0, 0)),
                  pl.BlockSpec((d, tn), lambda i, j: (0, j))],
        out_specs=pl.BlockSpec((tm, tn), lambda i, j: (i, j)),
        out_shape=jax.ShapeDtypeStruct((m, n), BF16),
        scratch_shapes=[pltpu.VMEM((tm, d), BF16)],
        compiler_params=_cparams(("arbitrary", "arbitrary")),
        name="in_proj",
    )(x2, nw, scale3, shift3, w_bf)


def _gates_kernel(x_ref, nw_ref, scale_ref, shift_ref, w_ref, wt_ref, alog_ref, dtb_ref,
                  alog_c_ref, dtb_c_ref, tri_ref, trit_ref, tok_ref, tim_ref, *, heads):
    h = _modulated_norm(x_ref[...], nw_ref[...], scale_ref[0], shift_ref[0]).astype(BF16)
    p = _dot(h, w_ref[...])
    lane = lax.broadcasted_iota(jnp.int32, p.shape, 1)
    g = -jnp.exp(alog_ref[...]) * _softplus(p + dtb_ref[...])
    gc = _dot_exact_lhs01(tri_ref[...], g)
    tok_ref[...] = jnp.where(lane < heads, _sigmoid(p), gc)
    pt = _dot_nt(wt_ref[...], h)
    row = lax.broadcasted_iota(jnp.int32, pt.shape, 0)
    gt = -jnp.exp(alog_c_ref[...]) * _softplus(pt + dtb_c_ref[...])
    gct = _dot_exact_rhs01(gt, trit_ref[...])
    tim_ref[...] = jnp.where(row < heads, _sigmoid(pt), gct)


def _gates_call(x2, nw, scale3, shift3, w_ba, w_ba_t, alog_row, dtb_row, alog_col, dtb_col,
                tri, tri_t, seq, heads):
    m, d = x2.shape
    tm = tri.shape[0]
    per_b = seq // tm
    batch = m // seq
    rows = w_ba_t.shape[0]
    const = lambda i: (0, 0)
    return pl.pallas_call(
        functools.partial(_gates_kernel, heads=heads),
        grid=(m // tm,),
        in_specs=[pl.BlockSpec((tm, d), lambda i: (i, 0)),
                  pl.BlockSpec((1, d), const),
                  pl.BlockSpec((1, 1, d), lambda i: (i // per_b, 0, 0)),
                  pl.BlockSpec((1, 1, d), lambda i: (i // per_b, 0, 0)),
                  pl.BlockSpec((d, HEAD_DIM), const),
                  pl.BlockSpec((rows, d), const),
                  pl.BlockSpec((1, HEAD_DIM), const),
                  pl.BlockSpec((1, HEAD_DIM), const),
                  pl.BlockSpec((rows, 1), const),
                  pl.BlockSpec((rows, 1), const),
                  pl.BlockSpec((tm, tm), const),
                  pl.BlockSpec((tm, tm), const)],
        out_specs=[pl.BlockSpec((tm, HEAD_DIM), lambda i: (i, 0)),
                   pl.BlockSpec((None, rows, tm), lambda i: (i // per_b, 0, i % per_b))],
        out_shape=[jax.ShapeDtypeStruct((m, HEAD_DIM), F32),
                   jax.ShapeDtypeStruct((batch, rows, seq), F32)],
        compiler_params=_cparams(("arbitrary",)),
        name="gdn_gates",
    )(x2, nw, scale3, shift3, w_ba, w_ba_t, alog_row, dtb_row, alog_col, dtb_col, tri, tri_t)


def _sb_kernel(q_ref, k_ref, v_ref, z_ref, u2_ref, u1_ref, o_ref, *, tq, scale):
    i = pl.program_id(2)
    t0 = i * tq
    ws = pl.multiple_of(jnp.maximum(t0 - tq, 0), tq)
    q = q_ref[...]

    kw = k_ref[pl.ds(ws, 2 * tq), :]
    vw = v_ref[pl.ds(ws, 2 * tq), :]
    zz = _dot_nt(q, kw) * scale
    sp = _softplus(zz)
    t_idx = t0 + lax.broadcasted_iota(jnp.int32, zz.shape, 0)
    s_idx = ws + lax.broadcasted_iota(jnp.int32, zz.shape, 1)
    mask = s_idx < t_idx
    spm = jnp.where(mask, sp, 0.0)
    later = _dot_exact_rhs01(spm, u2_ref[...])
    att = jnp.where(mask, jnp.exp(zz - sp - later), 0.0)
    acc0 = _dot(att.astype(BF16), vw)
    carry0 = jnp.sum(spm, axis=-1, keepdims=True)

    def cond(st):
        jb, carry, _ = st
        return jnp.logical_and(jb >= 0, jnp.min(carry) < SB_DEAD_LOG)

    def body(st):
        jb, carry, acc = st
        s0 = pl.multiple_of(jb * tq, tq)
        kb = k_ref[pl.ds(s0, tq), :]
        vb = v_ref[pl.ds(s0, tq), :]
        zb = _dot_nt(q, kb) * scale
        spb = _softplus(zb)
        lat = _dot_exact_rhs01(spb, u1_ref[...]) + carry
        ab = jnp.exp(zb - spb - lat)
        acc = acc + _dot(ab.astype(BF16), vb)
        carry = carry + jnp.sum(spb, axis=-1, keepdims=True)
        return jb - 1, carry, acc

    _, _, acc = lax.while_loop(cond, body, (ws // tq - 1, carry0, acc0))
    o_ref[...] = (acc * _silu(z_ref[...].astype(F32))).astype(o_ref.dtype)


def _sb_call(proj, u2, u1, batch, seq, heads):
    tq = u1.shape[0]
    nq = seq // tq
    kern = functools.partial(_sb_kernel, tq=tq, scale=HEAD_DIM ** -0.5)
    kv_spec = lambda seg: pl.BlockSpec((seq, HEAD_DIM), lambda b, h, i: (b, seg * heads + h))
    blk = lambda seg: pl.BlockSpec((tq, HEAD_DIM), lambda b, h, i: (b * nq + i, seg * heads + h))
    const = lambda b, h, i: (0, 0)
    return pl.pallas_call(
        kern,
        grid=(batch, heads, nq),
        in_specs=[blk(0), kv_spec(1), kv_spec(2), blk(3),
                  pl.BlockSpec((2 * tq, 2 * tq), const),
                  pl.BlockSpec((tq, tq), const)],
        out_specs=pl.BlockSpec((tq, HEAD_DIM), lambda b, h, i: (b * nq + i, h)),
        out_shape=jax.ShapeDtypeStruct((batch * seq, heads * HEAD_DIM), BF16),
        compiler_params=_cparams(("arbitrary", "arbitrary", "arbitrary")),
        name="stick_breaking",
    )(proj, proj, proj, proj, u2, u1)


def _unit_lower_inverse(low, masks):
    eye, m4, levels = masks
    d4 = low * m4
    d4b = d4.astype(BF16)
    p = _dot((eye - d4).astype(BF16), (eye + _dot(d4b, d4b)).astype(BF16))
    for ms in levels:
        cb = (low * ms).astype(BF16)
        pb = p.astype(BF16)
        p = p - _dot(_dot(pb, cb).astype(BF16), pb)
    return p


def _gdn_kernel(q_ref, k_ref, v_ref, z_ref, wq_ref, wk_ref, wv_ref, tok_ref, tim_ref, nw_ref,
                o_ref, xq_ref, xk_ref, xv_ref, s_ref, *, tb, heads):
    hd = pl.program_id(1)
    c = GDN_CHUNK
    n_chunks = tb // c

    @pl.when(pl.program_id(2) == 0)
    def _():
        s_ref[...] = jnp.zeros_like(s_ref)
        for x_ref in (xq_ref, xk_ref, xv_ref):
            x_ref[0:8, :] = jnp.zeros((8, HEAD_DIM), F32)

    def conv_silu(src_ref, x_ref, w_ref):
        x_ref[8:8 + tb, :] = src_ref[...].astype(F32)
        w = w_ref[...]
        acc = jnp.zeros((tb, HEAD_DIM), F32)
        for j in range(GDN_CONV):
            off = 8 - (GDN_CONV - 1) + j
            acc = acc + w[j:j + 1, :] * x_ref[off:off + tb, :]
        x_ref[0:8, :] = x_ref[tb:tb + 8, :]
        return _silu(acc)

    def l2n(a):
        return a * lax.rsqrt(jnp.sum(a * a, axis=-1, keepdims=True) + L2_EPS)

    qa = l2n(conv_silu(q_ref, xq_ref, wq_ref)) * (HEAD_DIM ** -0.5)
    ka = l2n(conv_silu(k_ref, xk_ref, wk_ref))
    va = conv_silu(v_ref, xv_ref, wv_ref)

    tok = tok_ref[...]
    lane = lax.broadcasted_iota(jnp.int32, tok.shape, 1)
    beta_c = jnp.sum(jnp.where(lane == hd, tok, 0.0), axis=-1, keepdims=True)
    gc_c = jnp.sum(jnp.where(lane == heads + hd, tok, 0.0), axis=-1, keepdims=True)
    gc_r = tim_ref[pl.ds(heads + hd, 1), :]

    ri = lax.broadcasted_iota(jnp.int32, (c, c), 0)
    ci = lax.broadcasted_iota(jnp.int32, (c, c), 1)
    tril = ri >= ci
    strict = ri > ci
    eye = jnp.where(ri == ci, 1.0, 0.0).astype(F32)
    m4 = jnp.where((ri >> 2) == (ci >> 2), 1.0, 0.0).astype(F32)
    levels = []
    for sh in range(2, c.bit_length() - 1):
        same_big = (ri >> (sh + 1)) == (ci >> (sh + 1))
        levels.append(jnp.where(same_big, jnp.where((ri >> sh) != (ci >> sh), 1.0, 0.0), 0.0))
    masks = (eye, m4, levels)
    nw = nw_ref[...]

    state = s_ref[...]
    for n in range(n_chunks):
        r0 = n * c
        q = qa[r0:r0 + c, :]
        k = ka[r0:r0 + c, :]
        v = va[r0:r0 + c, :]
        beta = beta_c[r0:r0 + c, :]
        g_c = gc_c[r0:r0 + c, :]
        g_r = gc_r[:, r0:r0 + c]
        g_last = g_c[c - 1:c, :]
        decay = jnp.exp(jnp.where(tril, g_c - g_r, -jnp.inf))
        kb = k * beta
        k16 = k.astype(BF16)
        low = jnp.where(strict, _dot_nt(kb.astype(BF16), k16) * decay, 0.0)
        t_mat = _unit_lower_inverse(low, masks).astype(BF16)
        eg = jnp.exp(g_c)
        u = _dot(t_mat, (v * beta).astype(BF16))
        w = _dot(t_mat, (kb * eg).astype(BF16))
        intra = jnp.where(tril, _dot_nt(q.astype(BF16), k16) * decay, 0.0)
        q_dec = q * eg
        k_dec = k * jnp.exp(g_last - g_c)
        s16 = state.astype(BF16)
        v_new = u - _dot(w.astype(BF16), s16)
        o = _dot(q_dec.astype(BF16), s16) + _dot(intra.astype(BF16), v_new.astype(BF16))
        state = state * jnp.exp(g_last) + lax.dot_general(
            k_dec.astype(BF16), v_new.astype(BF16), (((0,), (0,)), ((), ())),
            preferred_element_type=F32)
        on = o * lax.rsqrt(jnp.mean(o * o, axis=-1, keepdims=True) + NORM_EPS) * nw
        zg = z_ref[r0:r0 + c, :].astype(F32)
        o_ref[r0:r0 + c, :] = (on * _silu(zg)).astype(o_ref.dtype)
    s_ref[...] = state


def _gdn_call(proj, conv_w, tok, tim, gnw, batch, seq, heads, tb):
    nt = seq // tb
    rows = tim.shape[1]
    kern = functools.partial(_gdn_kernel, tb=tb, heads=heads)
    blk = lambda seg: pl.BlockSpec((tb, HEAD_DIM), lambda b, h, i: (b * nt + i, seg * heads + h))
    cw = lambda seg: pl.BlockSpec((GDN_CONV, HEAD_DIM), lambda b, h, i: (0, seg * heads + h))
    return pl.pallas_call(
        kern,
        grid=(batch, heads, nt),
        in_specs=[blk(4), blk(5), blk(6), blk(7), cw(0), cw(1), cw(2),
                  pl.BlockSpec((tb, HEAD_DIM), lambda b, h, i: (b * nt + i, 0)),
                  pl.BlockSpec((None, rows, tb), lambda b, h, i: (b, 0, i)),
                  pl.BlockSpec((1, HEAD_DIM), lambda b, h, i: (0, 0))],
        out_specs=pl.BlockSpec((tb, HEAD_DIM), lambda b, h, i: (b * nt + i, h)),
        out_shape=jax.ShapeDtypeStruct((batch * seq, heads * HEAD_DIM), BF16),
        scratch_shapes=[pltpu.VMEM((tb + 8, HEAD_DIM), F32),
                        pltpu.VMEM((tb + 8, HEAD_DIM), F32),
                        pltpu.VMEM((tb + 8, HEAD_DIM), F32),
                        pltpu.VMEM((HEAD_DIM, HEAD_DIM), F32)],
        compiler_params=_cparams(("arbitrary", "arbitrary", "arbitrary")),
        name="gated_deltanet",
    )(proj, proj, proj, proj, conv_w, conv_w, conv_w, tok, tim, gnw)


def _merge_kernel(osb_ref, ogdn_ref, wsb_ref, wgdn_ref, msb_ref, mgdn_ref, y_ref):
    a = _dot(osb_ref[...], wsb_ref[...])
    b = _dot(ogdn_ref[...], wgdn_ref[...])
    y = _sigmoid(msb_ref[...].astype(F32)) * a + _sigmoid(mgdn_ref[...].astype(F32)) * b
    y_ref[...] = y.astype(y_ref.dtype)


def _merge_call(o_sb, o_gdn, w_sb, w_gdn, proj, d):
    m, width = o_sb.shape
    tm = _tile(m, 1024)
    tn = _tile(d, 1024)
    nb = d // tn
    gate_base = 8 * d // tn
    return pl.pallas_call(
        _merge_kernel,
        grid=(nb, m // tm),
        in_specs=[pl.BlockSpec((tm, width), lambda j, i: (i, 0)),
                  pl.BlockSpec((tm, width), lambda j, i: (i, 0)),
                  pl.BlockSpec((width, tn), lambda j, i: (0, j)),
                  pl.BlockSpec((width, tn), lambda j, i: (0, j)),
                  pl.BlockSpec((tm, tn), lambda j, i: (i, gate_base + j)),
                  pl.BlockSpec((tm, tn), lambda j, i: (i, gate_base + nb + j))],
        out_specs=pl.BlockSpec((tm, tn), lambda j, i: (i, j)),
        out_shape=jax.ShapeDtypeStruct((m, d), BF16),
        compiler_params=_cparams(("arbitrary", "arbitrary")),
        name="gated_merge",
    )(o_sb, o_gdn, w_sb, w_gdn, proj, proj)


def _out_kernel(y_ref, w_ref, x_ref, gate_ref, fw_ref, o_ref):
    r = x_ref[...] + gate_ref[0] * _dot(y_ref[...], w_ref[...])
    ms = jnp.mean(r * r, axis=-1, keepdims=True)
    o_ref[...] = r * lax.rsqrt(ms + NORM_EPS) * fw_ref[...]


def _out_call(y, w_out, x2, gate3, fw, seq):
    m, d = x2.shape
    tm = _tile(seq, 512)
    per_b = seq // tm
    return pl.pallas_call(
        _out_kernel,
        grid=(m // tm,),
        in_specs=[pl.BlockSpec((tm, d), lambda i: (i, 0)),
                  pl.BlockSpec((d, d), lambda i: (0, 0)),
                  pl.BlockSpec((tm, d), lambda i: (i, 0)),
                  pl.BlockSpec((1, 1, d), lambda i: (i // per_b, 0, 0)),
                  pl.BlockSpec((1, d), lambda i: (0, 0))],
        out_specs=pl.BlockSpec((tm, d), lambda i: (i, 0)),
        out_shape=jax.ShapeDtypeStruct((m, d), F32),
        compiler_params=_cparams(("arbitrary",)),
        name="out_proj",
    )(y, w_out, x2, gate3, fw)


def _pad_lanes(a):
    return jnp.pad(a, ((0, 0), (0, HEAD_DIM - a.shape[1])))


def _layer(x2, c_pad, w_ada, b_ada, norm_w, w_in, conv_w, a_log, dt_bias, gdn_norm_w,
           w_proj_sb, w_proj_gdn, w_out, final_w, batch, seq):
    m, d = x2.shape
    heads = d // HEAD_DIM

    mod = _mod_call(c_pad, w_ada, b_ada.reshape(1, -1))[:batch]
    shift3 = mod[:, None, 0:d]
    scale3 = mod[:, None, d:2 * d]
    gate3 = mod[:, None, 2 * d:3 * d]
    nw = norm_w.reshape(1, d)

    w_main = jnp.concatenate([w_in[:, :8 * d], w_in[:, 8 * d + 2 * heads:]], axis=1).astype(BF16)
    w_ba = w_in[:, 8 * d:8 * d + 2 * heads]
    proj = _inproj_call(x2, nw, scale3, shift3, w_main, seq)

    tg = min(seq, 512)
    tpos = jnp.arange(tg)
    same_chunk = (tpos[:, None] // GDN_CHUNK) == (tpos[None, :] // GDN_CHUNK)
    tri = (same_chunk & (tpos[:, None] >= tpos[None, :])).astype(BF16)
    zero_h = jnp.zeros((heads,), F32)
    alog2 = jnp.concatenate([zero_h, a_log])
    dtb2 = jnp.concatenate([zero_h, dt_bias])
    tok, tim = _gates_call(
        x2, nw, scale3, shift3, _pad_lanes(w_ba).astype(BF16), w_ba.T.astype(BF16),
        _pad_lanes(alog2[None, :]), _pad_lanes(dtb2[None, :]), alog2[:, None], dtb2[:, None],
        tri, tri.T, seq, heads)

    tq = min(seq // 2, 256)
    kpos = jnp.arange(2 * tq)
    u2 = (kpos[:, None] > kpos[None, :]).astype(BF16)
    o_sb = _sb_call(proj, u2, u2[:tq, :tq], batch, seq, heads)

    o_gdn = _gdn_call(proj, conv_w, tok, tim, gdn_norm_w.reshape(1, HEAD_DIM), batch, seq, heads,
                      min(seq, 512))

    y = _merge_call(o_sb, o_gdn, w_proj_sb.astype(BF16), w_proj_gdn.astype(BF16), proj, d)
    return _out_call(y, w_out.astype(BF16), x2, gate3, final_w, seq)


def kernel(x, c, w_ada, b_ada, norm_w, w_in, gdn_conv_w, gdn_a_log, gdn_dt_bias, gdn_norm_w,
           w_proj_sb, w_proj_gdn, w_out, final_norm_w):
    batch, seq, d = x.shape
    depth = w_ada.shape[0]
    assert depth == 1, "the final rmsnorm is fused into the single layer's output kernel"
    x2 = x.reshape(batch * seq, d)
    c_pad = jnp.pad(c, ((0, 8 - batch), (0, 0)))
    out = _layer(x2, c_pad, w_ada[0], b_ada[0], norm_w[0], w_in[0], gdn_conv_w[0], gdn_a_log[0],
                 gdn_dt_bias[0], gdn_norm_w[0], w_proj_sb[0], w_proj_gdn[0], w_out[0],
                 final_norm_w.reshape(1, d), batch, seq)
    return out.reshape(batch, seq, d)
```

```python
import functools

import jax
import jax.numpy as jnp
from jax import lax
from jax.experimental import pallas as pl
from jax.experimental.pallas import tpu as pltpu

F32 = jnp.float32
BF16 = jnp.bfloat16

HEAD_DIM = 128
GDN_CHUNK = 64
GDN_CONV = 4
GDN_BLOCK = 512
GDN_HEAD_GROUP = 2
NORM_EPS = 1e-6
L2_EPS = 1e-6
V7X_VMEM_LIMIT = 56 * 1024 * 1024
SB_DEAD_LOG = 104.0
SB_BLOCK = 256
SB_SUBS = 2


def _tile(n, pref):
    t = min(n, pref)
    while n % t:
        t //= 2
    return t


def _cparams(sem):
    return pltpu.CompilerParams(dimension_semantics=sem, vmem_limit_bytes=V7X_VMEM_LIMIT)


def _softplus(x):
    return jnp.maximum(x, 0.0) + jnp.log(1.0 + jnp.exp(-jnp.abs(x)))


def _sigmoid(x):
    return 1.0 / (1.0 + jnp.exp(-x))


def _silu(x):
    return x * _sigmoid(x)


def _dot(a, b):
    return jnp.dot(a, b, preferred_element_type=F32)


def _dot_nt(a, b):
    return lax.dot_general(a, b, (((1,), (1,)), ((), ())), preferred_element_type=F32)


def _split3(a):
    hi = a.astype(BF16)
    r1 = a - hi.astype(F32)
    mid = r1.astype(BF16)
    lo = (r1 - mid.astype(F32)).astype(BF16)
    return hi, mid, lo


def _dot_exact_rhs01(a, m01):
    hi, mid, lo = _split3(a)
    return _dot(hi, m01) + _dot(mid, m01) + _dot(lo, m01)


def _dot_exact_lhs01(m01, a):
    hi, mid, lo = _split3(a)
    return _dot(m01, hi) + _dot(m01, mid) + _dot(m01, lo)


def _mod_kernel(c_ref, w_ref, b_ref, o_ref):
    a_hi, a_lo, _ = _split3(_silu(c_ref[...]))
    w_hi, w_lo, _ = _split3(w_ref[...])
    o_ref[...] = _dot(a_hi, w_hi) + _dot(a_hi, w_lo) + _dot(a_lo, w_hi) + b_ref[...]


def _mod_call(c_pad, w_ada, b_ada):
    rows, d = c_pad.shape
    n = w_ada.shape[1]
    tn = _tile(n, 768)
    return pl.pallas_call(
        _mod_kernel,
        grid=(n // tn,),
        in_specs=[pl.BlockSpec((rows, d), lambda j: (0, 0)),
                  pl.BlockSpec((d, tn), lambda j: (0, j)),
                  pl.BlockSpec((1, tn), lambda j: (0, j))],
        out_specs=pl.BlockSpec((rows, tn), lambda j: (0, j)),
        out_shape=jax.ShapeDtypeStruct((rows, n), F32),
        compiler_params=_cparams(("arbitrary",)),
        name="adaln_mod",
    )(c_pad, w_ada, b_ada)


def _modulated_norm(x, nw, scale, shift):
    ms = jnp.mean(x * x, axis=-1, keepdims=True)
    y = x * lax.rsqrt(ms + NORM_EPS) * nw
    return y * (1.0 + scale) + shift


def _inproj_kernel(x_ref, nw_ref, scale_ref, shift_ref, w_ref, o_ref, h_ref):
    @pl.when(pl.program_id(1) == 0)
    def _():
        h = _modulated_norm(x_ref[...], nw_ref[...], scale_ref[0], shift_ref[0])
        h_ref[...] = h.astype(BF16)

    o_ref[...] = _dot(h_ref[...], w_ref[...]).astype(o_ref.dtype)


def _inproj_call(x2, nw, scale3, shift3, w_bf, seq):
    m, d = x2.shape
    n = w_bf.shape[1]
    tm = _tile(seq, 1024)
    tn = _tile(n, 1024)
    per_b = seq // tm
    return pl.pallas_call(
        _inproj_kernel,
        grid=(m // tm, n // tn),
        in_specs=[pl.BlockSpec((tm, d), lambda i, j: (i, 0)),
                  pl.BlockSpec((1, d), lambda i, j: (0, 0)),
                  pl.BlockSpec((1, 1, d), lambda i, j: (i // per_b, 0, 0)),
                  pl.BlockSpec((1, 1, d), lambda i, j: (i // per_b, 0, 0)),
                  pl.BlockSpec((d, tn), lambda i, j: (0, j))],
        out_specs=pl.BlockSpec((tm, tn), lambda i, j: (i, j)),
        out_shape=jax.ShapeDtypeStruct((m, n), BF16),
        scratch_shapes=[pltpu.VMEM((tm, d), BF16)],
        compiler_params=_cparams(("arbitrary", "arbitrary")),
        name="in_proj",
    )(x2, nw, scale3, shift3, w_bf)


def _gates_kernel(x_ref, nw_ref, scale_ref, shift_ref, w_ref, wt_ref, alog_ref, dtb_ref,
                  alog_c_ref, dtb_c_ref, tri_ref, trit_ref, tok_ref, tim_ref, *, heads):
    h = _modulated_norm(x_ref[...], nw_ref[...], scale_ref[0], shift_ref[0]).astype(BF16)
    p = _dot(h, w_ref[...])
    lane = lax.broadcasted_iota(jnp.int32, p.shape, 1)
    g = -jnp.exp(alog_ref[...]) * _softplus(p + dtb_ref[...])
    gc = _dot_exact_lhs01(tri_ref[...], g)
    tok_ref[...] = jnp.where(lane < heads, _sigmoid(p), gc)
    pt = _dot_nt(wt_ref[...], h)
    row = lax.broadcasted_iota(jnp.int32, pt.shape, 0)
    gt = -jnp.exp(alog_c_ref[...]) * _softplus(pt + dtb_c_ref[...])
    gct = _dot_exact_rhs01(gt, trit_ref[...])
    tim_ref[...] = jnp.where(row < heads, _sigmoid(pt), gct)


def _gates_call(x2, nw, scale3, shift3, w_ba, w_ba_t, alog_row, dtb_row, alog_col, dtb_col,
                tri, tri_t, seq, heads):
    m, d = x2.shape
    tm = tri.shape[0]
    per_b = seq // tm
    batch = m // seq
    rows = w_ba_t.shape[0]
    const = lambda i: (0, 0)
    return pl.pallas_call(
        functools.partial(_gates_kernel, heads=heads),
        grid=(m // tm,),
        in_specs=[pl.BlockSpec((tm, d), lambda i: (i, 0)),
                  pl.BlockSpec((1, d), const),
                  pl.BlockSpec((1, 1, d), lambda i: (i // per_b, 0, 0)),
                  pl.BlockSpec((1, 1, d), lambda i: (i // per_b, 0, 0)),
                  pl.BlockSpec((d, HEAD_DIM), const),
                  pl.BlockSpec((rows, d), const),
                  pl.BlockSpec((1, HEAD_DIM), const),
                  pl.BlockSpec((1, HEAD_DIM), const),
                  pl.BlockSpec((rows, 1), const),
                  pl.BlockSpec((rows, 1), const),
                  pl.BlockSpec((tm, tm), const),
                  pl.BlockSpec((tm, tm), const)],
        out_specs=[pl.BlockSpec((tm, HEAD_DIM), lambda i: (i, 0)),
                   pl.BlockSpec((None, rows, tm), lambda i: (i // per_b, 0, i % per_b))],
        out_shape=[jax.ShapeDtypeStruct((m, HEAD_DIM), F32),
                   jax.ShapeDtypeStruct((batch, rows, seq), F32)],
        compiler_params=_cparams(("arbitrary",)),
        name="gdn_gates",
    )(x2, nw, scale3, shift3, w_ba, w_ba_t, alog_row, dtb_row, alog_col, dtb_col, tri, tri_t)


def _later_in_block(spm, u):
    hi = spm.astype(BF16)
    lo = (spm - hi.astype(F32)).astype(BF16)
    return _dot(hi, u) + _dot(lo, u)


def _sb_kernel(q_ref, k_ref, v_ref, z_ref, u_ref, o_ref, *, tq, subs, scale):
    i = pl.program_id(2)
    u = u_ref[...]
    rel = (lax.broadcasted_iota(jnp.int32, (tq, 2 * tq), 1)
           - lax.broadcasted_iota(jnp.int32, (tq, 2 * tq), 0))

    t0s = [(i * subs + s) * tq for s in range(subs)]
    wss = [pl.multiple_of(jnp.maximum(t0 - tq, 0), tq) for t0 in t0s]
    qs = [q_ref[s * tq:(s + 1) * tq, :] for s in range(subs)]
    kws = [k_ref[pl.ds(ws, 2 * tq), :] for ws in wss]
    vws = [v_ref[pl.ds(ws, 2 * tq), :] for ws in wss]
    zz = [_dot_nt(q, kw) * scale for q, kw in zip(qs, kws)]
    sp = _each(_softplus, zz)
    masks = [rel < (t0 - ws) for t0, ws in zip(t0s, wss)]
    spm = [jnp.where(m, a, 0.0) for m, a in zip(masks, sp)]
    cs_l = [_later_in_block(a[:, :tq], u) for a in spm]
    cs_r = [_later_in_block(a[:, tq:], u) for a in spm]
    tot_l = [jnp.sum(a[:, :tq], axis=-1, keepdims=True) for a in spm]
    tot_r = [jnp.sum(a[:, tq:], axis=-1, keepdims=True) for a in spm]
    accs, carries = [], []
    for s in range(subs):
        later = jnp.concatenate([cs_l[s] + tot_r[s], cs_r[s]], axis=1)
        att = jnp.where(masks[s], jnp.exp(zz[s] - sp[s] - later), 0.0)
        accs.append(_dot(att.astype(BF16), vws[s]))
        carries.append(tot_l[s] + tot_r[s])

    for s in range(subs):
        q = qs[s]

        def cond(st):
            jb, carry, _ = st
            return jnp.logical_and(jb >= 0, jnp.min(carry) < SB_DEAD_LOG)

        def body(st):
            jb, carry, acc = st
            s0 = pl.multiple_of(jb * tq, tq)
            zb = _dot_nt(q, k_ref[pl.ds(s0, tq), :]) * scale
            spb = _softplus(zb)
            ab = jnp.exp(zb - spb - _later_in_block(spb, u) - carry)
            acc = acc + _dot(ab.astype(BF16), v_ref[pl.ds(s0, tq), :])
            return jb - 1, carry + jnp.sum(spb, axis=-1, keepdims=True), acc

        _, _, acc = lax.while_loop(cond, body, (wss[s] // tq - 1, carries[s], accs[s]))
        zg = z_ref[s * tq:(s + 1) * tq, :].astype(F32)
        o_ref[s * tq:(s + 1) * tq, :] = (acc * _silu(zg)).astype(o_ref.dtype)


def _sb_call(proj, u, batch, seq, heads, subs):
    tq = u.shape[0]
    ts = tq * subs
    nq = seq // ts
    kern = functools.partial(_sb_kernel, tq=tq, subs=subs, scale=HEAD_DIM ** -0.5)
    kv_spec = lambda seg: pl.BlockSpec((seq, HEAD_DIM), lambda b, h, i: (b, seg * heads + h))
    blk = lambda seg: pl.BlockSpec((ts, HEAD_DIM), lambda b, h, i: (b * nq + i, seg * heads + h))
    return pl.pallas_call(
        kern,
        grid=(batch, heads, nq),
        in_specs=[blk(0), kv_spec(1), kv_spec(2), blk(3),
                  pl.BlockSpec((tq, tq), lambda b, h, i: (0, 0))],
        out_specs=pl.BlockSpec((ts, HEAD_DIM), lambda b, h, i: (b * nq + i, h)),
        out_shape=jax.ShapeDtypeStruct((batch * seq, heads * HEAD_DIM), BF16),
        compiler_params=_cparams(("arbitrary", "arbitrary", "arbitrary")),
        name="stick_breaking",
    )(proj, proj, proj, proj, u)


def _each(f, *lists):
    return [f(*a) for a in zip(*lists)]


def _b16(a):
    return a.astype(BF16)


def _unit_lower_inverse(low_list, eye, same8):
    d = [jnp.where(same8, a, 0.0) for a in low_list]
    e16 = [_b16(a - b) for a, b in zip(low_list, d)]
    d16 = _each(_b16, d)
    d2 = _each(_dot, d16, d16)
    d2_16 = _each(_b16, d2)
    d4 = _each(_dot, d2_16, d2_16)
    x1 = _each(lambda a, b: _dot(_b16(eye - a), _b16(eye + b)), d, d2)
    p8 = _each(lambda a, b: _dot(_b16(a), _b16(eye + b)), x1, d4)
    p8_16 = _each(_b16, p8)
    f = _each(_dot, p8_16, e16)
    f16 = _each(_b16, f)
    f2 = _each(_dot, f16, f16)
    f2_16 = _each(_b16, f2)
    f4 = _each(_dot, f2_16, f2_16)
    y1 = _each(lambda a, b: _dot(_b16(eye - a), _b16(eye + b)), f, f2)
    y = _each(lambda a, b: _dot(_b16(a), _b16(eye + b)), y1, f4)
    return _each(lambda a, b: _dot(_b16(a), b), y, p8_16)


def _gdn_kernel(q_ref, k_ref, v_ref, z_ref, wq_ref, wk_ref, wv_ref, tok_ref, tim_ref, nw_ref,
                o_ref, xq_ref, xk_ref, xv_ref, s_ref, *, tb, heads, group):
    hg = pl.program_id(1)
    c = GDN_CHUNK
    pt = 2 * c
    n_pairs = tb // pt
    width = group * HEAD_DIM

    @pl.when(pl.program_id(2) == 0)
    def _():
        s_ref[...] = jnp.zeros_like(s_ref)
        for x_ref in (xq_ref, xk_ref, xv_ref):
            x_ref[0:8, :] = jnp.zeros((8, width), F32)

    def conv_silu(src_ref, x_ref, w_ref):
        x_ref[8:8 + tb, :] = src_ref[...].astype(F32)
        w = w_ref[...]
        acc = jnp.zeros((tb, width), F32)
        for j in range(GDN_CONV):
            off = 8 - (GDN_CONV - 1) + j
            acc = acc + w[j:j + 1, :] * x_ref[off:off + tb, :]
        x_ref[0:8, :] = x_ref[tb:tb + 8, :]
        return _silu(acc)

    def l2n(a):
        return a * lax.rsqrt(jnp.sum(a * a, axis=-1, keepdims=True) + L2_EPS)

    q_all = conv_silu(q_ref, xq_ref, wq_ref)
    k_all = conv_silu(k_ref, xk_ref, wk_ref)
    v_all = conv_silu(v_ref, xv_ref, wv_ref)

    tok = tok_ref[...]
    lane = lax.broadcasted_iota(jnp.int32, (pt, HEAD_DIM), 1)
    ri = lax.broadcasted_iota(jnp.int32, (pt, pt), 0)
    ci = lax.broadcasted_iota(jnp.int32, (pt, pt), 1)
    same_chunk = (ri >> 6) == (ci >> 6)
    tril = jnp.logical_and(same_chunk, ri >= ci)
    strict = jnp.logical_and(same_chunk, ri > ci)
    same8 = (ri >> 3) == (ci >> 3)
    eye = jnp.where(ri == ci, 1.0, 0.0).astype(F32)
    first_col = lax.broadcasted_iota(jnp.int32, (pt, 1), 0) < c
    first_wide = lax.broadcasted_iota(jnp.int32, (pt, 2 * HEAD_DIM), 0) < c
    nw = nw_ref[...]

    units = [(g, m) for m in range(n_pairs) for g in range(group)]
    q_l, k16_l, vb_l, kbg_l, decay_l, gl_l, kdec_l, qdec_l, kb16_l = ([] for _ in range(9))
    for g, m in units:
        hd = hg * group + g
        r0 = m * pt
        cols = slice(g * HEAD_DIM, (g + 1) * HEAD_DIM)
        q = l2n(q_all[r0:r0 + pt, cols]) * (HEAD_DIM ** -0.5)
        k = l2n(k_all[r0:r0 + pt, cols])
        v = v_all[r0:r0 + pt, cols]
        tk = tok[r0:r0 + pt, :]
        ln = lane
        beta = jnp.sum(jnp.where(ln == hd, tk, 0.0), axis=-1, keepdims=True)
        g_c = jnp.sum(jnp.where(ln == heads + hd, tk, 0.0), axis=-1, keepdims=True)
        g_r = tim_ref[pl.ds(heads + hd, 1), :][:, r0:r0 + pt]
        g_end = [jnp.sum(jnp.where(ci == (half + 1) * c - 1, g_r, 0.0), axis=-1, keepdims=True)
                 for half in range(2)]
        g_last = jnp.where(first_col, g_end[0], g_end[1])
        eg = jnp.exp(g_c)
        kb = k * beta
        q_l.append(q)
        k16_l.append(_b16(k))
        kb16_l.append(_b16(kb))
        vb_l.append(_b16(v * beta))
        kbg_l.append(_b16(kb * eg))
        decay_l.append(jnp.exp(jnp.where(tril, g_c - g_r, -jnp.inf)))
        gl_l.append([jnp.exp(a) for a in g_end])
        kdec_l.append(k * jnp.exp(g_last - g_c))
        qdec_l.append(q * eg)

    kq = _each(lambda a, b, k16: _dot_nt(jnp.concatenate([a, _b16(b)], axis=0), k16),
               kb16_l, q_l, k16_l)
    low = _each(lambda a, dec: jnp.where(strict, a[:pt] * dec, 0.0), kq, decay_l)
    intra16 = _each(lambda a, dec: _b16(jnp.where(tril, a[pt:] * dec, 0.0)), kq, decay_l)
    t16 = _each(_b16, _unit_lower_inverse(low, eye, same8))
    uw16 = _each(lambda t, a, b: _b16(_dot(t, jnp.concatenate([a, b], axis=1))),
                 t16, vb_l, kbg_l)
    iw = _each(_dot, intra16, uw16)
    qp16 = _each(lambda qd, a: _b16(qd - a[:, HEAD_DIM:]), qdec_l, iw)
    kdt16 = _each(lambda a: _b16(a.T), kdec_l)
    zero16 = jnp.zeros((pt, 2 * HEAD_DIM), BF16)
    nk = [(_dot(a, jnp.where(first_wide, b, zero16)), _dot(a, jnp.where(first_wide, zero16, b)))
          for a, b in zip(kdt16, uw16)]

    states = [s_ref[g] for g in range(group)]
    for idx, (g, m) in enumerate(units):
        cols = slice(g * HEAD_DIM, (g + 1) * HEAD_DIM)
        for half in range(2):
            rows = slice(half * c, (half + 1) * c)
            r0 = m * pt + half * c
            s = states[g]
            lhs = jnp.concatenate([_b16(nk[idx][half][:, HEAD_DIM:]), qp16[idx][rows, :]], axis=0)
            ks_qs = _dot(lhs, _b16(s))
            o = ks_qs[HEAD_DIM:] + iw[idx][rows, :HEAD_DIM]
            states[g] = s * gl_l[idx][half] - ks_qs[:HEAD_DIM] + nk[idx][half][:, :HEAD_DIM]
            on = o * lax.rsqrt(jnp.mean(o * o, axis=-1, keepdims=True) + NORM_EPS) * nw
            zg = z_ref[r0:r0 + c, cols].astype(F32)
            o_ref[r0:r0 + c, cols] = (on * _silu(zg)).astype(o_ref.dtype)
    for g in range(group):
        s_ref[g] = states[g]


def _gdn_call(proj, conv_w, tok, tim, gnw, batch, seq, heads, tb, group):
    nt = seq // tb
    rows = tim.shape[1]
    ng = heads // group
    width = group * HEAD_DIM
    kern = functools.partial(_gdn_kernel, tb=tb, heads=heads, group=group)
    blk = lambda seg: pl.BlockSpec((tb, width), lambda b, h, i: (b * nt + i, seg * ng + h))
    cw = lambda seg: pl.BlockSpec((GDN_CONV, width), lambda b, h, i: (0, seg * ng + h))
    return pl.pallas_call(
        kern,
        grid=(batch, ng, nt),
        in_specs=[blk(4), blk(5), blk(6), blk(7), cw(0), cw(1), cw(2),
                  pl.BlockSpec((tb, HEAD_DIM), lambda b, h, i: (b * nt + i, 0)),
                  pl.BlockSpec((None, rows, tb), lambda b, h, i: (b, 0, i)),
                  pl.BlockSpec((1, HEAD_DIM), lambda b, h, i: (0, 0))],
        out_specs=pl.BlockSpec((tb, width), lambda b, h, i: (b * nt + i, h)),
        out_shape=jax.ShapeDtypeStruct((batch * seq, heads * HEAD_DIM), BF16),
        scratch_shapes=[pltpu.VMEM((tb + 8, width), F32),
                        pltpu.VMEM((tb + 8, width), F32),
                        pltpu.VMEM((tb + 8, width), F32),
                        pltpu.VMEM((group, HEAD_DIM, HEAD_DIM), F32)],
        compiler_params=_cparams(("arbitrary", "arbitrary", "arbitrary")),
        name="gated_deltanet",
    )(proj, proj, proj, proj, conv_w, conv_w, conv_w, tok, tim, gnw)


def _merge_kernel(osb_ref, ogdn_ref, wsb_ref, wgdn_ref, msb_ref, mgdn_ref, y_ref):
    a = _dot(osb_ref[...], wsb_ref[...])
    b = _dot(ogdn_ref[...], wgdn_ref[...])
    y = _sigmoid(msb_ref[...].astype(F32)) * a + _sigmoid(mgdn_ref[...].astype(F32)) * b
    y_ref[...] = y.astype(y_ref.dtype)


def _merge_call(o_sb, o_gdn, w_sb, w_gdn, proj, d):
    m, width = o_sb.shape
    tm = _tile(m, 1024)
    tn = _tile(d, 1024)
    nb = d // tn
    gate_base = 8 * d // tn
    return pl.pallas_call(
        _merge_kernel,
        grid=(nb, m // tm),
        in_specs=[pl.BlockSpec((tm, width), lambda j, i: (i, 0)),
                  pl.BlockSpec((tm, width), lambda j, i: (i, 0)),
                  pl.BlockSpec((width, tn), lambda j, i: (0, j)),
                  pl.BlockSpec((width, tn), lambda j, i: (0, j)),
                  pl.BlockSpec((tm, tn), lambda j, i: (i, gate_base + j)),
                  pl.BlockSpec((tm, tn), lambda j, i: (i, gate_base + nb + j))],
        out_specs=pl.BlockSpec((tm, tn), lambda j, i: (i, j)),
        out_shape=jax.ShapeDtypeStruct((m, d), BF16),
        compiler_params=_cparams(("arbitrary", "arbitrary")),
        name="gated_merge",
    )(o_sb, o_gdn, w_sb, w_gdn, proj, proj)


def _out_kernel(y_ref, w_ref, x_ref, gate_ref, fw_ref, o_ref):
    r = x_ref[...] + gate_ref[0] * _dot(y_ref[...], w_ref[...])
    ms = jnp.mean(r * r, axis=-1, keepdims=True)
    o_ref[...] = r * lax.rsqrt(ms + NORM_EPS) * fw_ref[...]


def _out_call(y, w_out, x2, gate3, fw, seq):
    m, d = x2.shape
    tm = _tile(seq, 512)
    per_b = seq // tm
    return pl.pallas_call(
        _out_kernel,
        grid=(m // tm,),
        in_specs=[pl.BlockSpec((tm, d), lambda i: (i, 0)),
                  pl.BlockSpec((d, d), lambda i: (0, 0)),
                  pl.BlockSpec((tm, d), lambda i: (i, 0)),
                  pl.BlockSpec((1, 1, d), lambda i: (i // per_b, 0, 0)),
                  pl.BlockSpec((1, d), lambda i: (0, 0))],
        out_specs=pl.BlockSpec((tm, d), lambda i: (i, 0)),
        out_shape=jax.ShapeDtypeStruct((m, d), F32),
        compiler_params=_cparams(("arbitrary",)),
        name="out_proj",
    )(y, w_out, x2, gate3, fw)


def _pad_lanes(a):
    return jnp.pad(a, ((0, 0), (0, HEAD_DIM - a.shape[1])))


def _layer(x2, c_pad, w_ada, b_ada, norm_w, w_in, conv_w, a_log, dt_bias, gdn_norm_w,
           w_proj_sb, w_proj_gdn, w_out, final_w, batch, seq):
    m, d = x2.shape
    heads = d // HEAD_DIM

    mod = _mod_call(c_pad, w_ada, b_ada.reshape(1, -1))[:batch]
    shift3 = mod[:, None, 0:d]
    scale3 = mod[:, None, d:2 * d]
    gate3 = mod[:, None, 2 * d:3 * d]
    nw = norm_w.reshape(1, d)

    w_main = jnp.concatenate([w_in[:, :8 * d], w_in[:, 8 * d + 2 * heads:]], axis=1).astype(BF16)
    w_ba = w_in[:, 8 * d:8 * d + 2 * heads]
    proj = _inproj_call(x2, nw, scale3, shift3, w_main, seq)

    tg = min(seq, 512)
    tpos = jnp.arange(tg)
    same_chunk = (tpos[:, None] // GDN_CHUNK) == (tpos[None, :] // GDN_CHUNK)
    tri = (same_chunk & (tpos[:, None] >= tpos[None, :])).astype(BF16)
    zero_h = jnp.zeros((heads,), F32)
    alog2 = jnp.concatenate([zero_h, a_log])
    dtb2 = jnp.concatenate([zero_h, dt_bias])
    tok, tim = _gates_call(
        x2, nw, scale3, shift3, _pad_lanes(w_ba).astype(BF16), w_ba.T.astype(BF16),
        _pad_lanes(alog2[None, :]), _pad_lanes(dtb2[None, :]), alog2[:, None], dtb2[:, None],
        tri, tri.T, seq, heads)

    tq = _tile(seq // 2, SB_BLOCK)
    kpos = jnp.arange(tq)
    upper = (kpos[:, None] > kpos[None, :]).astype(BF16)
    o_sb = _sb_call(proj, upper, batch, seq, heads, min(SB_SUBS, seq // tq))

    o_gdn = _gdn_call(proj, conv_w, tok, tim, gdn_norm_w.reshape(1, HEAD_DIM), batch, seq, heads,
                      _tile(seq, GDN_BLOCK), GDN_HEAD_GROUP)

    y = _merge_call(o_sb, o_gdn, w_proj_sb.astype(BF16), w_proj_gdn.astype(BF16), proj, d)
    return _out_call(y, w_out.astype(BF16), x2, gate3, final_w, seq)


def kernel(x, c, w_ada, b_ada, norm_w, w_in, gdn_conv_w, gdn_a_log, gdn_dt_bias, gdn_norm_w,
           w_proj_sb, w_proj_gdn, w_out, final_norm_w):
    batch, seq, d = x.shape
    depth = w_ada.shape[0]
    assert depth == 1, "the final rmsnorm is fused into the single layer's output kernel"
    x2 = x.reshape(batch * seq, d)
    c_pad = jnp.pad(c, ((0, 8 - batch), (0, 0)))
    out = _layer(x2, c_pad, w_ada[0], b_ada[0], norm_w[0], w_in[0], gdn_conv_w[0], gdn_a_log[0],
                 gdn_dt_bias[0], gdn_norm_w[0], w_proj_sb[0], w_proj_gdn[0], w_out[0],
                 final_norm_w.reshape(1, d), batch, seq)
    return out.reshape(batch, seq, d)
```

```python
import functools

import jax
import jax.numpy as jnp
from jax import lax
from jax.experimental import pallas as pl
from jax.experimental.pallas import tpu as pltpu

F32 = jnp.float32
BF16 = jnp.bfloat16

HEAD_DIM = 128
GDN_CHUNK = 64
GDN_CONV = 4
GDN_BLOCK = 512
GDN_HEAD_GROUP = 4
NORM_EPS = 1e-6
L2_EPS = 1e-6
V7X_VMEM_LIMIT = 56 * 1024 * 1024
SB_DEAD_LOG = 104.0
SB_QUERY_BLOCK = 128
SB_KEY_BLOCK = 256
SB_SUBS = 4
LOG2_E = 1.4426950408889634


def _tile(n, pref):
    t = min(n, pref)
    while n % t:
        t //= 2
    return t


def _cparams(sem):
    return pltpu.CompilerParams(dimension_semantics=sem, vmem_limit_bytes=V7X_VMEM_LIMIT)


def _softplus(x):
    return jnp.maximum(x, 0.0) + jnp.log(1.0 + jnp.exp(-jnp.abs(x)))


def _sigmoid(x):
    return 1.0 / (1.0 + jnp.exp(-x))


def _silu(x):
    return x * _sigmoid(x)


def _dot(a, b):
    return jnp.dot(a, b, preferred_element_type=F32)


def _dot_nt(a, b):
    return lax.dot_general(a, b, (((1,), (1,)), ((), ())), preferred_element_type=F32)


def _split3(a):
    hi = a.astype(BF16)
    r1 = a - hi.astype(F32)
    mid = r1.astype(BF16)
    lo = (r1 - mid.astype(F32)).astype(BF16)
    return hi, mid, lo


def _dot_exact_rhs01(a, m01):
    hi, mid, lo = _split3(a)
    return _dot(hi, m01) + _dot(mid, m01) + _dot(lo, m01)


def _dot_exact_lhs01(m01, a):
    hi, mid, lo = _split3(a)
    return _dot(m01, hi) + _dot(m01, mid) + _dot(m01, lo)


def _mod_kernel(c_ref, w_ref, b_ref, o_ref):
    a_hi, a_lo, _ = _split3(_silu(c_ref[...]))
    w_hi, w_lo, _ = _split3(w_ref[...])
    o_ref[...] = _dot(a_hi, w_hi) + _dot(a_hi, w_lo) + _dot(a_lo, w_hi) + b_ref[...]


def _mod_call(c_pad, w_ada, b_ada):
    rows, d = c_pad.shape
    n = w_ada.shape[1]
    tn = _tile(n, 768)
    return pl.pallas_call(
        _mod_kernel,
        grid=(n // tn,),
        in_specs=[pl.BlockSpec((rows, d), lambda j: (0, 0)),
                  pl.BlockSpec((d, tn), lambda j: (0, j)),
                  pl.BlockSpec((1, tn), lambda j: (0, j))],
        out_specs=pl.BlockSpec((rows, tn), lambda j: (0, j)),
        out_shape=jax.ShapeDtypeStruct((rows, n), F32),
        compiler_params=_cparams(("arbitrary",)),
        name="adaln_mod",
    )(c_pad, w_ada, b_ada)


def _modulated_norm(x, nw, scale, shift):
    ms = jnp.mean(x * x, axis=-1, keepdims=True)
    y = x * lax.rsqrt(ms + NORM_EPS) * nw
    return y * (1.0 + scale) + shift


def _inproj_kernel(x_ref, nw_ref, scale_ref, shift_ref, wa_ref, wb_ref, o_ref, h_ref, *, na):
    j = pl.program_id(1)

    @pl.when(j == 0)
    def _():
        h = _modulated_norm(x_ref[...], nw_ref[...], scale_ref[0], shift_ref[0])
        h_ref[...] = h.astype(BF16)

    @pl.when(j < na)
    def _():
        o_ref[...] = _dot(h_ref[...], wa_ref[...]).astype(o_ref.dtype)

    @pl.when(j >= na)
    def _():
        o_ref[...] = _dot(h_ref[...], wb_ref[...]).astype(o_ref.dtype)


def _inproj_call(x2, nw, scale3, shift3, w_a, w_b, seq):
    m, d = x2.shape
    tm = _tile(seq, 1024)
    tn = _tile(w_b.shape[1], 1024)
    na, nb = w_a.shape[1] // tn, w_b.shape[1] // tn
    per_b = seq // tm
    return pl.pallas_call(
        functools.partial(_inproj_kernel, na=na),
        grid=(m // tm, na + nb),
        in_specs=[pl.BlockSpec((tm, d), lambda i, j: (i, 0)),
                  pl.BlockSpec((1, d), lambda i, j: (0, 0)),
                  pl.BlockSpec((1, 1, d), lambda i, j: (i // per_b, 0, 0)),
                  pl.BlockSpec((1, 1, d), lambda i, j: (i // per_b, 0, 0)),
                  pl.BlockSpec((d, tn), lambda i, j: (0, jnp.minimum(j, na - 1))),
                  pl.BlockSpec((d, tn), lambda i, j: (0, jnp.maximum(j - na, 0)))],
        out_specs=pl.BlockSpec((tm, tn), lambda i, j: (i, j)),
        out_shape=jax.ShapeDtypeStruct((m, (na + nb) * tn), BF16),
        scratch_shapes=[pltpu.VMEM((tm, d), BF16)],
        compiler_params=_cparams(("arbitrary", "arbitrary")),
        name="in_proj",
    )(x2, nw, scale3, shift3, w_a, w_b)


def _gates_kernel(x_ref, nw_ref, scale_ref, shift_ref, w_ref, alog_ref, dtb_ref,
                  alog_c_ref, dtb_c_ref, tri_ref, trit_ref, tok_ref, tim_ref, *, heads):
    h = _modulated_norm(x_ref[...], nw_ref[...], scale_ref[0], shift_ref[0]).astype(BF16)
    p = _dot(h, w_ref[...].astype(BF16))
    lane = lax.broadcasted_iota(jnp.int32, p.shape, 1)
    g = -jnp.exp(alog_ref[...]) * _softplus(p + dtb_ref[...])
    gc = _dot_exact_lhs01(tri_ref[...], g)
    tok_ref[...] = jnp.where(lane < heads, _sigmoid(p), gc)
    pt = p.T[:2 * heads, :]
    row = lax.broadcasted_iota(jnp.int32, pt.shape, 0)
    gt = -jnp.exp(alog_c_ref[...]) * _softplus(pt + dtb_c_ref[...])
    gct = _dot_exact_rhs01(gt, trit_ref[...])
    tim_ref[...] = jnp.where(row < heads, _sigmoid(pt), gct)


def _gates_call(x2, nw, scale3, shift3, w_in, ba_col, alog_row, dtb_row, alog_col, dtb_col,
                tri, tri_t, seq, heads):
    m, d = x2.shape
    tm = tri.shape[0]
    per_b = seq // tm
    batch = m // seq
    rows = 2 * heads
    const = lambda i: (0, 0)
    return pl.pallas_call(
        functools.partial(_gates_kernel, heads=heads),
        grid=(m // tm,),
        in_specs=[pl.BlockSpec((tm, d), lambda i: (i, 0)),
                  pl.BlockSpec((1, d), const),
                  pl.BlockSpec((1, 1, d), lambda i: (i // per_b, 0, 0)),
                  pl.BlockSpec((1, 1, d), lambda i: (i // per_b, 0, 0)),
                  pl.BlockSpec((d, HEAD_DIM), lambda i: (0, ba_col // HEAD_DIM)),
                  pl.BlockSpec((1, HEAD_DIM), const),
                  pl.BlockSpec((1, HEAD_DIM), const),
                  pl.BlockSpec((rows, 1), const),
                  pl.BlockSpec((rows, 1), const),
                  pl.BlockSpec((tm, tm), const),
                  pl.BlockSpec((tm, tm), const)],
        out_specs=[pl.BlockSpec((tm, HEAD_DIM), lambda i: (i, 0)),
                   pl.BlockSpec((None, rows, tm), lambda i: (i // per_b, 0, i % per_b))],
        out_shape=[jax.ShapeDtypeStruct((m, HEAD_DIM), F32),
                   jax.ShapeDtypeStruct((batch, rows, seq), F32)],
        compiler_params=_cparams(("arbitrary",)),
        name="gdn_gates",
    )(x2, nw, scale3, shift3, w_in, alog_row, dtb_row, alog_col, dtb_col, tri, tri_t)


def _later_in_block(spm, u):
    hi = spm.astype(BF16)
    lo = (spm - hi.astype(F32)).astype(BF16)
    return _dot(hi, u) + _dot(lo, u)


def _softplus2(zs):
    return jnp.maximum(zs, 0.0) + jnp.log2(1.0 + jnp.exp2(jnp.minimum(zs, -zs)))


def _sb_kernel(q_ref, k_ref, v_ref, z_ref, u_ref, o_ref, acc_ref, *, tq, subs, scale):
    i = pl.program_id(2)
    u = u_ref[...]
    kb = u.shape[0]
    win = 2 * kb
    c1 = scale * LOG2_E
    dead = SB_DEAD_LOG * LOG2_E
    rel = (lax.broadcasted_iota(jnp.int32, (tq, win), 1)
           - lax.broadcasted_iota(jnp.int32, (tq, win), 0))

    t0s = [(i * subs + s) * tq for s in range(subs)]
    wss = [pl.multiple_of(jnp.maximum(t0 + tq - win, 0), tq) for t0 in t0s]
    qs = [q_ref[s * tq:(s + 1) * tq, :] for s in range(subs)]
    kws = [k_ref[pl.ds(ws, win), :] for ws in wss]
    vws = [v_ref[pl.ds(ws, win), :] for ws in wss]
    zs = [_dot_nt(q, kw) * c1 for q, kw in zip(qs, kws)]
    sp = _each(_softplus2, zs)
    masks = [rel < (t0 - ws) for t0, ws in zip(t0s, wss)]
    spm = [jnp.where(m, a, 0.0) for m, a in zip(masks, sp)]
    cs_l = [_later_in_block(a[:, :kb], u) for a in spm]
    cs_r = [_later_in_block(a[:, kb:], u) for a in spm]
    tot_l = [jnp.sum(a[:, :kb], axis=-1, keepdims=True) for a in spm]
    tot_r = [jnp.sum(a[:, kb:], axis=-1, keepdims=True) for a in spm]
    carries = []
    for s in range(subs):
        later = jnp.concatenate([cs_l[s] + tot_r[s], cs_r[s]], axis=1)
        att = jnp.where(masks[s], jnp.exp2(zs[s] - sp[s] - later), 0.0)
        acc_ref[s * tq:(s + 1) * tq, :] = _dot(att.astype(BF16), vws[s])
        carries.append(tot_l[s] + tot_r[s])

    least = carries[0]
    for a in carries[1:]:
        least = jnp.minimum(least, a)

    @pl.when(jnp.min(least) < dead)
    def _():
        u_small = u[:tq, :tq]
        for s in range(subs):
            q = qs[s]

            def cond(st):
                jb, carry = st
                return jnp.logical_and(jb >= 0, jnp.min(carry) < dead)

            def body(st):
                jb, carry = st
                s0 = pl.multiple_of(jb * tq, tq)
                zb = _dot_nt(q, k_ref[pl.ds(s0, tq), :]) * c1
                spb = _softplus2(zb)
                ab = jnp.exp2(zb - spb - _later_in_block(spb, u_small) - carry)
                acc_ref[s * tq:(s + 1) * tq, :] += _dot(ab.astype(BF16), v_ref[pl.ds(s0, tq), :])
                return jb - 1, carry + jnp.sum(spb, axis=-1, keepdims=True)

            lax.while_loop(cond, body, (wss[s] // tq - 1, carries[s]))

    o_ref[...] = (acc_ref[...] * _silu(z_ref[...].astype(F32))).astype(o_ref.dtype)


def _sb_call(proj, u, batch, seq, heads, tq, subs):
    ts = tq * subs
    nq = seq // ts
    kern = functools.partial(_sb_kernel, tq=tq, subs=subs, scale=HEAD_DIM ** -0.5)
    kv_spec = lambda seg: pl.BlockSpec((seq, HEAD_DIM), lambda b, h, i: (b, seg * heads + h))
    blk = lambda seg: pl.BlockSpec((ts, HEAD_DIM), lambda b, h, i: (b * nq + i, seg * heads + h))
    return pl.pallas_call(
        kern,
        grid=(batch, heads, nq),
        in_specs=[blk(0), kv_spec(1), kv_spec(2), blk(3),
                  pl.BlockSpec(u.shape, lambda b, h, i: (0, 0))],
        out_specs=pl.BlockSpec((ts, HEAD_DIM), lambda b, h, i: (b * nq + i, h)),
        out_shape=jax.ShapeDtypeStruct((batch * seq, heads * HEAD_DIM), BF16),
        scratch_shapes=[pltpu.VMEM((ts, HEAD_DIM), F32)],
        compiler_params=_cparams(("arbitrary", "arbitrary", "arbitrary")),
        name="stick_breaking",
    )(proj, proj, proj, proj, u)


def _each(f, *lists):
    return [f(*a) for a in zip(*lists)]


def _b16(a):
    return a.astype(BF16)


def _unit_lower_inverse(low_list, eye, same8):
    d = [jnp.where(same8, a, 0.0) for a in low_list]
    e16 = [_b16(a - b) for a, b in zip(low_list, d)]
    d16 = _each(_b16, d)
    d2 = _each(_dot, d16, d16)
    d2_16 = _each(_b16, d2)
    d4 = _each(_dot, d2_16, d2_16)
    x1 = _each(lambda a, b: _dot(_b16(eye - a), _b16(eye + b)), d, d2)
    p8 = _each(lambda a, b: _dot(_b16(a), _b16(eye + b)), x1, d4)
    p8_16 = _each(_b16, p8)
    f = _each(_dot, p8_16, e16)
    f16 = _each(_b16, f)
    f2 = _each(_dot, f16, f16)
    f2_16 = _each(_b16, f2)
    f4 = _each(_dot, f2_16, f2_16)
    y1 = _each(lambda a, b: _dot(_b16(eye - a), _b16(eye + b)), f, f2)
    y = _each(lambda a, b: _dot(_b16(a), _b16(eye + b)), y1, f4)
    return _each(lambda a, b: _dot(_b16(a), b), y, p8_16)


def _gdn_kernel(q_ref, k_ref, v_ref, z_ref, xq_ref, xk_ref, xv_ref, wq_ref, wk_ref, wv_ref,
                tok_ref, tim_ref, nw_ref, shift_ref, o_ref, s_ref, *, tb, heads, group):
    hg = pl.program_id(1)
    c = GDN_CHUNK
    pt = 2 * c
    n_pairs = tb // pt
    width = group * HEAD_DIM

    first_block = pl.program_id(2) == 0

    @pl.when(first_block)
    def _():
        s_ref[...] = jnp.zeros_like(s_ref)

    def conv_silu(src_ref, before_ref, w_ref):
        w = w_ref[...]
        before = before_ref[...]
        before = jnp.where(first_block, jnp.zeros_like(before), before)
        tiles = []
        for m in range(n_pairs):
            cur = src_ref[m * pt:(m + 1) * pt, :]
            prev = before if m == 0 else src_ref[(m - 1) * pt:m * pt, :]
            shifted = _dot(shift_ref[...], jnp.concatenate([prev, cur], axis=0))
            acc = w[0:1, :] * shifted[0:pt, :]
            for j in range(1, GDN_CONV - 1):
                acc = acc + w[j:j + 1, :] * shifted[j * pt:(j + 1) * pt, :]
            acc = acc + w[GDN_CONV - 1:GDN_CONV, :] * cur.astype(F32)
            tiles.append(_silu(acc))
        return tiles

    def l2n(a):
        return a * lax.rsqrt(jnp.sum(a * a, axis=-1, keepdims=True) + L2_EPS)

    q_tiles = conv_silu(q_ref, xq_ref, wq_ref)
    k_tiles = conv_silu(k_ref, xk_ref, wk_ref)
    v_tiles = conv_silu(v_ref, xv_ref, wv_ref)

    tok = tok_ref[...]
    lane = lax.broadcasted_iota(jnp.int32, (pt, HEAD_DIM), 1)
    ri = lax.broadcasted_iota(jnp.int32, (pt, pt), 0)
    ci = lax.broadcasted_iota(jnp.int32, (pt, pt), 1)
    same_chunk = (ri >> 6) == (ci >> 6)
    tril = jnp.logical_and(same_chunk, ri >= ci)
    strict = jnp.logical_and(same_chunk, ri > ci)
    same8 = (ri >> 3) == (ci >> 3)
    eye = jnp.where(ri == ci, 1.0, 0.0).astype(F32)
    first_col = lax.broadcasted_iota(jnp.int32, (pt, 1), 0) < c
    first_wide = lax.broadcasted_iota(jnp.int32, (pt, 2 * HEAD_DIM), 0) < c
    nw = nw_ref[...]

    units = [(g, m) for m in range(n_pairs) for g in range(group)]
    q_l, k16_l, vb_l, kbg_l, decay_l, gl_l, kdec_l, qdec_l, kb16_l = ([] for _ in range(9))
    for g, m in units:
        hd = hg * group + g
        r0 = m * pt
        cols = slice(g * HEAD_DIM, (g + 1) * HEAD_DIM)
        q = l2n(q_tiles[m][:, cols]) * (HEAD_DIM ** -0.5)
        k = l2n(k_tiles[m][:, cols])
        v = v_tiles[m][:, cols]
        tk = tok[r0:r0 + pt, :]
        ln = lane
        beta = jnp.sum(jnp.where(ln == hd, tk, 0.0), axis=-1, keepdims=True)
        g_c = jnp.sum(jnp.where(ln == heads + hd, tk, 0.0), axis=-1, keepdims=True)
        g_r = tim_ref[pl.ds(heads + hd, 1), :][:, r0:r0 + pt]
        g_end = [jnp.sum(jnp.where(ci == (half + 1) * c - 1, g_r, 0.0), axis=-1, keepdims=True)
                 for half in range(2)]
        g_last = jnp.where(first_col, g_end[0], g_end[1])
        eg = jnp.exp(g_c)
        kb = k * beta
        q_l.append(q)
        k16_l.append(_b16(k))
        kb16_l.append(_b16(kb))
        vb_l.append(_b16(v * beta))
        kbg_l.append(_b16(kb * eg))
        decay_l.append(jnp.exp(jnp.where(tril, g_c - g_r, -jnp.inf)))
        gl_l.append([jnp.exp(a) for a in g_end])
        kdec_l.append(k * jnp.exp(g_last - g_c))
        qdec_l.append(q * eg)

    kq = _each(lambda a, b, k16: _dot_nt(jnp.concatenate([a, _b16(b)], axis=0), k16),
               kb16_l, q_l, k16_l)
    low = _each(lambda a, dec: jnp.where(strict, a[:pt] * dec, 0.0), kq, decay_l)
    intra16 = _each(lambda a, dec: _b16(jnp.where(tril, a[pt:] * dec, 0.0)), kq, decay_l)
    t16 = _each(_b16, _unit_lower_inverse(low, eye, same8))
    uw16 = _each(lambda t, a, b: _b16(_dot(t, jnp.concatenate([a, b], axis=1))),
                 t16, vb_l, kbg_l)
    iw = _each(_dot, intra16, uw16)
    qp16 = _each(lambda qd, a: _b16(qd - a[:, HEAD_DIM:]), qdec_l, iw)
    kdt16 = _each(lambda a: _b16(a.T), kdec_l)
    zero16 = jnp.zeros((pt, 2 * HEAD_DIM), BF16)
    nk = [(_dot(a, jnp.where(first_wide, b, zero16)), _dot(a, jnp.where(first_wide, zero16, b)))
          for a, b in zip(kdt16, uw16)]

    states = [s_ref[g] for g in range(group)]
    for idx, (g, m) in enumerate(units):
        cols = slice(g * HEAD_DIM, (g + 1) * HEAD_DIM)
        for half in range(2):
            rows = slice(half * c, (half + 1) * c)
            r0 = m * pt + half * c
            s = states[g]
            lhs = jnp.concatenate([_b16(nk[idx][half][:, HEAD_DIM:]), qp16[idx][rows, :]], axis=0)
            ks_qs = _dot(lhs, _b16(s))
            o = ks_qs[HEAD_DIM:] + iw[idx][rows, :HEAD_DIM]
            states[g] = s * gl_l[idx][half] - ks_qs[:HEAD_DIM] + nk[idx][half][:, :HEAD_DIM]
            on = o * lax.rsqrt(jnp.mean(o * o, axis=-1, keepdims=True) + NORM_EPS) * nw
            zg = z_ref[r0:r0 + c, cols].astype(F32)
            o_ref[r0:r0 + c, cols] = (on * _silu(zg)).astype(o_ref.dtype)
    for g in range(group):
        s_ref[g] = states[g]


def _gdn_call(proj, conv_w, tok, tim, gnw, batch, seq, heads, tb, group):
    nt = seq // tb
    rows = tim.shape[1]
    ng = heads // group
    width = group * HEAD_DIM
    kern = functools.partial(_gdn_kernel, tb=tb, heads=heads, group=group)
    blk = lambda seg: pl.BlockSpec((tb, width), lambda b, h, i: (b * nt + i, seg * ng + h))
    cw = lambda seg: pl.BlockSpec((GDN_CONV, width), lambda b, h, i: (0, seg * ng + h))
    pt = 2 * GDN_CHUNK
    out_row = jnp.arange((GDN_CONV - 1) * pt)
    src_row = pt + out_row % pt - (GDN_CONV - 1) + out_row // pt
    shift = (src_row[:, None] == jnp.arange(2 * pt)[None, :]).astype(BF16)
    per_blk = tb // pt
    before = lambda seg: pl.BlockSpec(
        (pt, width), lambda b, h, i: (jnp.maximum((b * nt + i) * per_blk - 1, 0), seg * ng + h))
    return pl.pallas_call(
        kern,
        grid=(batch, ng, nt),
        in_specs=[blk(4), blk(5), blk(6), blk(7), before(4), before(5), before(6),
                  cw(0), cw(1), cw(2),
                  pl.BlockSpec((tb, HEAD_DIM), lambda b, h, i: (b * nt + i, 0)),
                  pl.BlockSpec((None, rows, tb), lambda b, h, i: (b, 0, i)),
                  pl.BlockSpec((1, HEAD_DIM), lambda b, h, i: (0, 0)),
                  pl.BlockSpec(shift.shape, lambda b, h, i: (0, 0))],
        out_specs=pl.BlockSpec((tb, width), lambda b, h, i: (b * nt + i, h)),
        out_shape=jax.ShapeDtypeStruct((batch * seq, heads * HEAD_DIM), BF16),
        scratch_shapes=[pltpu.VMEM((group, HEAD_DIM, HEAD_DIM), F32)],
        compiler_params=_cparams(("arbitrary", "arbitrary", "arbitrary")),
        name="gated_deltanet",
    )(proj, proj, proj, proj, proj, proj, proj, conv_w, conv_w, conv_w, tok, tim, gnw, shift)


def _merge_kernel(osb_ref, ogdn_ref, wsb_ref, wgdn_ref, msb_ref, mgdn_ref, y_ref):
    a = _dot(osb_ref[...], wsb_ref[...])
    b = _dot(ogdn_ref[...], wgdn_ref[...])
    y = _sigmoid(msb_ref[...].astype(F32)) * a + _sigmoid(mgdn_ref[...].astype(F32)) * b
    y_ref[...] = y.astype(y_ref.dtype)


def _merge_call(o_sb, o_gdn, w_sb, w_gdn, proj, d):
    m, width = o_sb.shape
    tm = _tile(m, 1024)
    tn = _tile(d, 1024)
    nb = d // tn
    gate_base = 8 * d // tn
    return pl.pallas_call(
        _merge_kernel,
        grid=(nb, m // tm),
        in_specs=[pl.BlockSpec((tm, width), lambda j, i: (i, 0)),
                  pl.BlockSpec((tm, width), lambda j, i: (i, 0)),
                  pl.BlockSpec((width, tn), lambda j, i: (0, j)),
                  pl.BlockSpec((width, tn), lambda j, i: (0, j)),
                  pl.BlockSpec((tm, tn), lambda j, i: (i, gate_base + j)),
                  pl.BlockSpec((tm, tn), lambda j, i: (i, gate_base + nb + j))],
        out_specs=pl.BlockSpec((tm, tn), lambda j, i: (i, j)),
        out_shape=jax.ShapeDtypeStruct((m, d), BF16),
        compiler_params=_cparams(("arbitrary", "arbitrary")),
        name="gated_merge",
    )(o_sb, o_gdn, w_sb, w_gdn, proj, proj)


def _out_kernel(y_ref, w_ref, x_ref, gate_ref, fw_ref, o_ref):
    r = x_ref[...] + gate_ref[0] * _dot(y_ref[...], w_ref[...])
    ms = jnp.mean(r * r, axis=-1, keepdims=True)
    o_ref[...] = r * lax.rsqrt(ms + NORM_EPS) * fw_ref[...]


def _out_call(y, w_out, x2, gate3, fw, seq):
    m, d = x2.shape
    tm = _tile(seq, 512)
    per_b = seq // tm
    return pl.pallas_call(
        _out_kernel,
        grid=(m // tm,),
        in_specs=[pl.BlockSpec((tm, d), lambda i: (i, 0)),
                  pl.BlockSpec((d, d), lambda i: (0, 0)),
                  pl.BlockSpec((tm, d), lambda i: (i, 0)),
                  pl.BlockSpec((1, 1, d), lambda i: (i // per_b, 0, 0)),
                  pl.BlockSpec((1, d), lambda i: (0, 0))],
        out_specs=pl.BlockSpec((tm, d), lambda i: (i, 0)),
        out_shape=jax.ShapeDtypeStruct((m, d), F32),
        compiler_params=_cparams(("arbitrary",)),
        name="out_proj",
    )(y, w_out, x2, gate3, fw)


def _pad_lanes(a):
    return jnp.pad(a, ((0, 0), (0, HEAD_DIM - a.shape[1])))


def _layer(x2, c_pad, w_ada, b_ada, norm_w, w_in, conv_w, a_log, dt_bias, gdn_norm_w,
           w_proj_sb, w_proj_gdn, w_out, final_w, batch, seq):
    m, d = x2.shape
    heads = d // HEAD_DIM

    mod = _mod_call(c_pad, w_ada, b_ada.reshape(1, -1))[:batch]
    shift3 = mod[:, None, 0:d]
    scale3 = mod[:, None, d:2 * d]
    gate3 = mod[:, None, 2 * d:3 * d]
    nw = norm_w.reshape(1, d)

    proj = _inproj_call(x2, nw, scale3, shift3, w_in[:, :8 * d].astype(BF16),
                        w_in[:, 8 * d + 2 * heads:].astype(BF16), seq)

    tg = min(seq, 512)
    tpos = jnp.arange(tg)
    same_chunk = (tpos[:, None] // GDN_CHUNK) == (tpos[None, :] // GDN_CHUNK)
    tri = (same_chunk & (tpos[:, None] >= tpos[None, :])).astype(BF16)
    zero_h = jnp.zeros((heads,), F32)
    alog2 = jnp.concatenate([zero_h, a_log])
    dtb2 = jnp.concatenate([zero_h, dt_bias])
    tok, tim = _gates_call(
        x2, nw, scale3, shift3, w_in, 8 * d,
        _pad_lanes(alog2[None, :]), _pad_lanes(dtb2[None, :]), alog2[:, None], dtb2[:, None],
        tri, tri.T, seq, heads)

    kpos = jnp.arange(_tile(seq // 2, SB_KEY_BLOCK))
    upper = (kpos[:, None] > kpos[None, :]).astype(BF16)
    tq = min(SB_QUERY_BLOCK, upper.shape[0])
    o_sb = _sb_call(proj, upper, batch, seq, heads, tq, min(SB_SUBS, seq // tq))

    o_gdn = _gdn_call(proj, conv_w, tok, tim, gdn_norm_w.reshape(1, HEAD_DIM), batch, seq, heads,
                      _tile(seq, GDN_BLOCK), _tile(heads, GDN_HEAD_GROUP))

    y = _merge_call(o_sb, o_gdn, w_proj_sb.astype(BF16), w_proj_gdn.astype(BF16), proj, d)
    return _out_call(y, w_out.astype(BF16), x2, gate3, final_w, seq)


def kernel(x, c, w_ada, b_ada, norm_w, w_in, gdn_conv_w, gdn_a_log, gdn_dt_bias, gdn_norm_w,
           w_proj_sb, w_proj_gdn, w_out, final_norm_w):
    batch, seq, d = x.shape
    depth = w_ada.shape[0]
    assert depth == 1, "the final rmsnorm is fused into the single layer's output kernel"
    x2 = x.reshape(batch * seq, d)
    c_pad = jnp.pad(c, ((0, 8 - batch), (0, 0)))
    out = _layer(x2, c_pad, w_ada[0], b_ada[0], norm_w[0], w_in[0], gdn_conv_w[0], gdn_a_log[0],
                 gdn_dt_bias[0], gdn_norm_w[0], w_proj_sb[0], w_proj_gdn[0], w_out[0],
                 final_norm_w.reshape(1, d), batch, seq)
    return out.reshape(batch, seq, d)
```

```python
import functools

import jax
import jax.numpy as jnp
from jax import lax
from jax.experimental import pallas as pl
from jax.experimental.pallas import tpu as pltpu

F32 = jnp.float32
BF16 = jnp.bfloat16

HEAD_DIM = 128
GDN_CHUNK = 64
GDN_CONV = 4
GDN_BLOCK = 512
GDN_HEAD_GROUP = 4
NORM_EPS = 1e-6
L2_EPS = 1e-6
V7X_VMEM_LIMIT = 56 * 1024 * 1024
SB_DEAD_LOG = 104.0
SB_QUERY_BLOCK = 128
SB_KEY_BLOCK = 256
SB_SUBS = 8
LOG2_E = 1.4426950408889634


def _tile(n, pref):
    t = min(n, pref)
    while n % t:
        t //= 2
    return t


def _cparams(sem):
    return pltpu.CompilerParams(dimension_semantics=sem, vmem_limit_bytes=V7X_VMEM_LIMIT)


def _softplus(x):
    return jnp.maximum(x, 0.0) + jnp.log(1.0 + jnp.exp(-jnp.abs(x)))


def _sigmoid(x):
    return 1.0 / (1.0 + jnp.exp(-x))


def _silu(x):
    return x * _sigmoid(x)


def _dot(a, b):
    return jnp.dot(a, b, preferred_element_type=F32)


def _dot_nt(a, b):
    return lax.dot_general(a, b, (((1,), (1,)), ((), ())), preferred_element_type=F32)


def _split3(a):
    hi = a.astype(BF16)
    r1 = a - hi.astype(F32)
    mid = r1.astype(BF16)
    lo = (r1 - mid.astype(F32)).astype(BF16)
    return hi, mid, lo


def _dot_exact_rhs01(a, m01):
    hi, mid, lo = _split3(a)
    return _dot(hi, m01) + _dot(mid, m01) + _dot(lo, m01)


def _dot_exact_lhs01(m01, a):
    hi, mid, lo = _split3(a)
    return _dot(m01, hi) + _dot(m01, mid) + _dot(m01, lo)


def _mod_kernel(c_ref, w_ref, b_ref, o_ref):
    a_hi, a_lo, _ = _split3(_silu(c_ref[...]))
    w_hi, w_lo, _ = _split3(w_ref[...])
    o_ref[...] = _dot(a_hi, w_hi) + _dot(a_hi, w_lo) + _dot(a_lo, w_hi) + b_ref[...]


def _mod_call(c_pad, w_ada, b_ada):
    rows, d = c_pad.shape
    n = w_ada.shape[1]
    tn = _tile(n, 768)
    return pl.pallas_call(
        _mod_kernel,
        grid=(n // tn,),
        in_specs=[pl.BlockSpec((rows, d), lambda j: (0, 0)),
                  pl.BlockSpec((d, tn), lambda j: (0, j)),
                  pl.BlockSpec((1, tn), lambda j: (0, j))],
        out_specs=pl.BlockSpec((rows, tn), lambda j: (0, j)),
        out_shape=jax.ShapeDtypeStruct((rows, n), F32),
        compiler_params=_cparams(("arbitrary",)),
        name="adaln_mod",
    )(c_pad, w_ada, b_ada)


def _modulated_norm(x, nw, scale, shift):
    ms = jnp.mean(x * x, axis=-1, keepdims=True)
    y = x * lax.rsqrt(ms + NORM_EPS) * nw
    return y * (1.0 + scale) + shift


def _inproj_kernel(x_ref, nw_ref, scale_ref, shift_ref, wa_ref, wb_ref, o_ref, h_ref, *, na):
    j = pl.program_id(1)

    @pl.when(j == 0)
    def _():
        h = _modulated_norm(x_ref[...], nw_ref[...], scale_ref[0], shift_ref[0])
        h_ref[...] = h.astype(BF16)

    @pl.when(j < na)
    def _():
        o_ref[...] = _dot(h_ref[...], wa_ref[...]).astype(o_ref.dtype)

    @pl.when(j >= na)
    def _():
        o_ref[...] = _dot(h_ref[...], wb_ref[...]).astype(o_ref.dtype)


def _inproj_call(x2, nw, scale3, shift3, w_a, w_b, seq):
    m, d = x2.shape
    tm = _tile(seq, 1024)
    tn = _tile(w_b.shape[1], 1024)
    na, nb = w_a.shape[1] // tn, w_b.shape[1] // tn
    per_b = seq // tm
    return pl.pallas_call(
        functools.partial(_inproj_kernel, na=na),
        grid=(m // tm, na + nb),
        in_specs=[pl.BlockSpec((tm, d), lambda i, j: (i, 0)),
                  pl.BlockSpec((1, d), lambda i, j: (0, 0)),
                  pl.BlockSpec((1, 1, d), lambda i, j: (i // per_b, 0, 0)),
                  pl.BlockSpec((1, 1, d), lambda i, j: (i // per_b, 0, 0)),
                  pl.BlockSpec((d, tn), lambda i, j: (0, jnp.minimum(j, na - 1))),
                  pl.BlockSpec((d, tn), lambda i, j: (0, jnp.maximum(j - na, 0)))],
        out_specs=pl.BlockSpec((tm, tn), lambda i, j: (i, j)),
        out_shape=jax.ShapeDtypeStruct((m, (na + nb) * tn), BF16),
        scratch_shapes=[pltpu.VMEM((tm, d), BF16)],
        compiler_params=_cparams(("arbitrary", "arbitrary")),
        name="in_proj",
    )(x2, nw, scale3, shift3, w_a, w_b)


def _gates_kernel(x_ref, nw_ref, scale_ref, shift_ref, w_ref, alog_ref, dtb_ref,
                  alog_c_ref, dtb_c_ref, tri_ref, trit_ref, tok_ref, tim_ref, *, heads):
    h = _modulated_norm(x_ref[...], nw_ref[...], scale_ref[0], shift_ref[0]).astype(BF16)
    p = _dot(h, w_ref[...].astype(BF16))
    lane = lax.broadcasted_iota(jnp.int32, p.shape, 1)
    g = -jnp.exp(alog_ref[...]) * _softplus(p + dtb_ref[...])
    gc = _dot_exact_lhs01(tri_ref[...], g)
    tok_ref[...] = jnp.where(lane < heads, _sigmoid(p), gc)
    pt = p.T[:2 * heads, :]
    row = lax.broadcasted_iota(jnp.int32, pt.shape, 0)
    gt = -jnp.exp(alog_c_ref[...]) * _softplus(pt + dtb_c_ref[...])
    gct = _dot_exact_rhs01(gt, trit_ref[...])
    tim_ref[...] = jnp.where(row < heads, _sigmoid(pt), gct)


def _gates_call(x2, nw, scale3, shift3, w_in, ba_col, alog_row, dtb_row, alog_col, dtb_col,
                tri, tri_t, seq, heads):
    m, d = x2.shape
    tm = tri.shape[0]
    per_b = seq // tm
    batch = m // seq
    rows = 2 * heads
    const = lambda i: (0, 0)
    return pl.pallas_call(
        functools.partial(_gates_kernel, heads=heads),
        grid=(m // tm,),
        in_specs=[pl.BlockSpec((tm, d), lambda i: (i, 0)),
                  pl.BlockSpec((1, d), const),
                  pl.BlockSpec((1, 1, d), lambda i: (i // per_b, 0, 0)),
                  pl.BlockSpec((1, 1, d), lambda i: (i // per_b, 0, 0)),
                  pl.BlockSpec((d, HEAD_DIM), lambda i: (0, ba_col // HEAD_DIM)),
                  pl.BlockSpec((1, HEAD_DIM), const),
                  pl.BlockSpec((1, HEAD_DIM), const),
                  pl.BlockSpec((rows, 1), const),
                  pl.BlockSpec((rows, 1), const),
                  pl.BlockSpec((tm, tm), const),
                  pl.BlockSpec((tm, tm), const)],
        out_specs=[pl.BlockSpec((tm, HEAD_DIM), lambda i: (i, 0)),
                   pl.BlockSpec((None, rows, tm), lambda i: (i // per_b, 0, i % per_b))],
        out_shape=[jax.ShapeDtypeStruct((m, HEAD_DIM), F32),
                   jax.ShapeDtypeStruct((batch, rows, seq), F32)],
        compiler_params=_cparams(("arbitrary",)),
        name="gdn_gates",
    )(x2, nw, scale3, shift3, w_in, alog_row, dtb_row, alog_col, dtb_col, tri, tri_t)


def _later_in_block(spm, u):
    hi = spm.astype(BF16)
    lo = (spm - hi.astype(F32)).astype(BF16)
    return _dot(hi, u) + _dot(lo, u)


def _softplus2(zs):
    return jnp.maximum(zs, 0.0) + jnp.log2(1.0 + jnp.exp2(jnp.minimum(zs, -zs)))


def _sb_kernel(q_ref, k_ref, v_ref, z_ref, u_ref, o_ref, acc_ref, *, tq, subs, scale):
    i = pl.program_id(2)
    u = u_ref[...]
    kb = u.shape[0]
    win = 2 * kb
    c1 = scale * LOG2_E
    dead = SB_DEAD_LOG * LOG2_E
    rel = (lax.broadcasted_iota(jnp.int32, (tq, win), 1)
           - lax.broadcasted_iota(jnp.int32, (tq, win), 0))

    t0s = [(i * subs + s) * tq for s in range(subs)]
    wss = [pl.multiple_of(jnp.maximum(t0 + tq - win, 0), tq) for t0 in t0s]
    qs = [q_ref[s * tq:(s + 1) * tq, :] for s in range(subs)]
    kws = [k_ref[pl.ds(ws, win), :] for ws in wss]
    vws = [v_ref[pl.ds(ws, win), :] for ws in wss]
    zs = [_dot_nt(q, kw) * c1 for q, kw in zip(qs, kws)]
    sp = _each(_softplus2, zs)
    masks = [rel < (t0 - ws) for t0, ws in zip(t0s, wss)]
    spm = [jnp.where(m, a, 0.0) for m, a in zip(masks, sp)]
    cs_l = [_later_in_block(a[:, :kb], u) for a in spm]
    cs_r = [_later_in_block(a[:, kb:], u) for a in spm]
    tot_l = [jnp.sum(a[:, :kb], axis=-1, keepdims=True) for a in spm]
    tot_r = [jnp.sum(a[:, kb:], axis=-1, keepdims=True) for a in spm]
    carries = []
    for s in range(subs):
        later = jnp.concatenate([cs_l[s] + tot_r[s], cs_r[s]], axis=1)
        att = jnp.where(masks[s], jnp.exp2(zs[s] - sp[s] - later), 0.0)
        acc_ref[s * tq:(s + 1) * tq, :] = _dot(att.astype(BF16), vws[s])
        carries.append(tot_l[s] + tot_r[s])

    least = carries[0]
    for a in carries[1:]:
        least = jnp.minimum(least, a)

    @pl.when(jnp.min(least) < dead)
    def _():
        u_small = u[:tq, :tq]
        alive = [jnp.min(a) < dead for a in carries]

        def older_keys(s):
            q = qs[s]

            def cond(st):
                jb, carry = st
                return jnp.logical_and(jb >= 0, jnp.min(carry) < dead)

            def body(st):
                jb, carry = st
                s0 = pl.multiple_of(jb * tq, tq)
                zb = _dot_nt(q, k_ref[pl.ds(s0, tq), :]) * c1
                spb = _softplus2(zb)
                ab = jnp.exp2(zb - spb - _later_in_block(spb, u_small) - carry)
                acc_ref[s * tq:(s + 1) * tq, :] += _dot(ab.astype(BF16), v_ref[pl.ds(s0, tq), :])
                return jb - 1, carry + jnp.sum(spb, axis=-1, keepdims=True)

            lax.while_loop(cond, body, (wss[s] // tq - 1, carries[s]))

        for s in range(subs):
            pl.when(alive[s])(functools.partial(older_keys, s))

    o_ref[...] = (acc_ref[...] * _silu(z_ref[...].astype(F32))).astype(o_ref.dtype)


def _sb_call(proj, u, batch, seq, heads, tq, subs):
    ts = tq * subs
    nq = seq // ts
    kern = functools.partial(_sb_kernel, tq=tq, subs=subs, scale=HEAD_DIM ** -0.5)
    kv_spec = lambda seg: pl.BlockSpec((seq, HEAD_DIM), lambda b, h, i: (b, seg * heads + h))
    blk = lambda seg: pl.BlockSpec((ts, HEAD_DIM), lambda b, h, i: (b * nq + i, seg * heads + h))
    return pl.pallas_call(
        kern,
        grid=(batch, heads, nq),
        in_specs=[blk(0), kv_spec(1), kv_spec(2), blk(3),
                  pl.BlockSpec(u.shape, lambda b, h, i: (0, 0))],
        out_specs=pl.BlockSpec((ts, HEAD_DIM), lambda b, h, i: (b * nq + i, h)),
        out_shape=jax.ShapeDtypeStruct((batch * seq, heads * HEAD_DIM), BF16),
        scratch_shapes=[pltpu.VMEM((ts, HEAD_DIM), F32)],
        compiler_params=_cparams(("arbitrary", "arbitrary", "arbitrary")),
        name="stick_breaking",
    )(proj, proj, proj, proj, u)


def _each(f, *lists):
    return [f(*a) for a in zip(*lists)]


def _b16(a):
    return a.astype(BF16)


def _unit_lower_inverse(low_list, eye, same8):
    d = [jnp.where(same8, a, 0.0) for a in low_list]
    e16 = [_b16(a - b) for a, b in zip(low_list, d)]
    d16 = _each(_b16, d)
    d2 = _each(_dot, d16, d16)
    d2_16 = _each(_b16, d2)
    d4 = _each(_dot, d2_16, d2_16)
    x1 = _each(lambda a, b: _dot(_b16(eye - a), _b16(eye + b)), d, d2)
    p8 = _each(lambda a, b: _dot(_b16(a), _b16(eye + b)), x1, d4)
    p8_16 = _each(_b16, p8)
    f = _each(_dot, p8_16, e16)
    f16 = _each(_b16, f)
    f2 = _each(_dot, f16, f16)
    f2_16 = _each(_b16, f2)
    f4 = _each(_dot, f2_16, f2_16)
    y1 = _each(lambda a, b: _dot(_b16(eye - a), _b16(eye + b)), f, f2)
    y = _each(lambda a, b: _dot(_b16(a), _b16(eye + b)), y1, f4)
    return _each(lambda a, b: _dot(_b16(a), b), y, p8_16)


def _gdn_kernel(q_ref, k_ref, v_ref, z_ref, xq_ref, xk_ref, xv_ref, wq_ref, wk_ref, wv_ref,
                tok_ref, tim_ref, nw_ref, shift_ref, o_ref, s_ref, *, tb, heads, group):
    hg = pl.program_id(1)
    c = GDN_CHUNK
    pt = 2 * c
    n_pairs = tb // pt
    width = group * HEAD_DIM

    first_block = pl.program_id(2) == 0

    @pl.when(first_block)
    def _():
        s_ref[...] = jnp.zeros_like(s_ref)

    def conv_silu(src_ref, before_ref, w_ref):
        w = w_ref[...]
        before = before_ref[...]
        before = jnp.where(first_block, jnp.zeros_like(before), before)
        tiles = []
        for m in range(n_pairs):
            cur = src_ref[m * pt:(m + 1) * pt, :]
            prev = before if m == 0 else src_ref[(m - 1) * pt:m * pt, :]
            shifted = _dot(shift_ref[...], jnp.concatenate([prev, cur], axis=0))
            acc = w[0:1, :] * shifted[0:pt, :]
            for j in range(1, GDN_CONV - 1):
                acc = acc + w[j:j + 1, :] * shifted[j * pt:(j + 1) * pt, :]
            acc = acc + w[GDN_CONV - 1:GDN_CONV, :] * cur.astype(F32)
            tiles.append(_silu(acc))
        return tiles

    def l2n(a):
        return a * lax.rsqrt(jnp.sum(a * a, axis=-1, keepdims=True) + L2_EPS)

    q_tiles = conv_silu(q_ref, xq_ref, wq_ref)
    k_tiles = conv_silu(k_ref, xk_ref, wk_ref)
    v_tiles = conv_silu(v_ref, xv_ref, wv_ref)

    tok = tok_ref[...]
    lane = lax.broadcasted_iota(jnp.int32, (pt, HEAD_DIM), 1)
    ri = lax.broadcasted_iota(jnp.int32, (pt, pt), 0)
    ci = lax.broadcasted_iota(jnp.int32, (pt, pt), 1)
    same_chunk = (ri >> 6) == (ci >> 6)
    tril = jnp.logical_and(same_chunk, ri >= ci)
    strict = jnp.logical_and(same_chunk, ri > ci)
    same8 = (ri >> 3) == (ci >> 3)
    eye = jnp.where(ri == ci, 1.0, 0.0).astype(F32)
    first_col = lax.broadcasted_iota(jnp.int32, (pt, 1), 0) < c
    first_wide = lax.broadcasted_iota(jnp.int32, (pt, 2 * HEAD_DIM), 0) < c
    nw = nw_ref[...]

    units = [(g, m) for m in range(n_pairs) for g in range(group)]
    q_l, k16_l, vb_l, kbg_l, decay_l, gl_l, kdec_l, qdec_l, kb16_l = ([] for _ in range(9))
    for g, m in units:
        hd = hg * group + g
        r0 = m * pt
        cols = slice(g * HEAD_DIM, (g + 1) * HEAD_DIM)
        q = l2n(q_tiles[m][:, cols]) * (HEAD_DIM ** -0.5)
        k = l2n(k_tiles[m][:, cols])
        v = v_tiles[m][:, cols]
        tk = tok[r0:r0 + pt, :]
        ln = lane
        beta = jnp.sum(jnp.where(ln == hd, tk, 0.0), axis=-1, keepdims=True)
        g_c = jnp.sum(jnp.where(ln == heads + hd, tk, 0.0), axis=-1, keepdims=True)
        g_r = tim_ref[pl.ds(heads + hd, 1), :][:, r0:r0 + pt]
        g_end = [jnp.sum(jnp.where(ci == (half + 1) * c - 1, g_r, 0.0), axis=-1, keepdims=True)
                 for half in range(2)]
        g_last = jnp.where(first_col, g_end[0], g_end[1])
        eg = jnp.exp(g_c)
        kb = k * beta
        q_l.append(q)
        k16_l.append(_b16(k))
        kb16_l.append(_b16(kb))
        vb_l.append(_b16(v * beta))
        kbg_l.append(_b16(kb * eg))
        decay_l.append(jnp.exp(jnp.where(tril, g_c - g_r, -jnp.inf)))
        gl_l.append([jnp.exp(a) for a in g_end])
        kdec_l.append(k * jnp.exp(g_last - g_c))
        qdec_l.append(q * eg)

    kq = _each(lambda a, b, k16: _dot_nt(jnp.concatenate([a, _b16(b)], axis=0), k16),
               kb16_l, q_l, k16_l)
    low = _each(lambda a, dec: jnp.where(strict, a[:pt] * dec, 0.0), kq, decay_l)
    intra16 = _each(lambda a, dec: _b16(jnp.where(tril, a[pt:] * dec, 0.0)), kq, decay_l)
    t16 = _each(_b16, _unit_lower_inverse(low, eye, same8))
    uw16 = _each(lambda t, a, b: _b16(_dot(t, jnp.concatenate([a, b], axis=1))),
                 t16, vb_l, kbg_l)
    iw = _each(_dot, intra16, uw16)
    qp16 = _each(lambda qd, a: _b16(qd - a[:, HEAD_DIM:]), qdec_l, iw)
    kdt16 = _each(lambda a: _b16(a.T), kdec_l)
    zero16 = jnp.zeros((pt, 2 * HEAD_DIM), BF16)
    nk = [(_dot(a, jnp.where(first_wide, b, zero16)), _dot(a, jnp.where(first_wide, zero16, b)))
          for a, b in zip(kdt16, uw16)]

    states = [s_ref[g] for g in range(group)]
    for idx, (g, m) in enumerate(units):
        cols = slice(g * HEAD_DIM, (g + 1) * HEAD_DIM)
        for half in range(2):
            rows = slice(half * c, (half + 1) * c)
            r0 = m * pt + half * c
            s = states[g]
            lhs = jnp.concatenate([_b16(nk[idx][half][:, HEAD_DIM:]), qp16[idx][rows, :]], axis=0)
            ks_qs = _dot(lhs, _b16(s))
            o = ks_qs[HEAD_DIM:] + iw[idx][rows, :HEAD_DIM]
            states[g] = s * gl_l[idx][half] - ks_qs[:HEAD_DIM] + nk[idx][half][:, :HEAD_DIM]
            on = o * lax.rsqrt(jnp.mean(o * o, axis=-1, keepdims=True) + NORM_EPS) * nw
            zg = z_ref[r0:r0 + c, cols].astype(F32)
            o_ref[r0:r0 + c, cols] = (on * _silu(zg)).astype(o_ref.dtype)
    for g in range(group):
        s_ref[g] = states[g]


def _gdn_call(proj, conv_w, tok, tim, gnw, batch, seq, heads, tb, group):
    nt = seq // tb
    rows = tim.shape[1]
    ng = heads // group
    width = group * HEAD_DIM
    kern = functools.partial(_gdn_kernel, tb=tb, heads=heads, group=group)
    blk = lambda seg: pl.BlockSpec((tb, width), lambda b, h, i: (b * nt + i, seg * ng + h))
    cw = lambda seg: pl.BlockSpec((GDN_CONV, width), lambda b, h, i: (0, seg * ng + h))
    pt = 2 * GDN_CHUNK
    out_row = jnp.arange((GDN_CONV - 1) * pt)
    src_row = pt + out_row % pt - (GDN_CONV - 1) + out_row // pt
    shift = (src_row[:, None] == jnp.arange(2 * pt)[None, :]).astype(BF16)
    per_blk = tb // pt
    before = lambda seg: pl.BlockSpec(
        (pt, width), lambda b, h, i: (jnp.maximum((b * nt + i) * per_blk - 1, 0), seg * ng + h))
    return pl.pallas_call(
        kern,
        grid=(batch, ng, nt),
        in_specs=[blk(4), blk(5), blk(6), blk(7), before(4), before(5), before(6),
                  cw(0), cw(1), cw(2),
                  pl.BlockSpec((tb, HEAD_DIM), lambda b, h, i: (b * nt + i, 0)),
                  pl.BlockSpec((None, rows, tb), lambda b, h, i: (b, 0, i)),
                  pl.BlockSpec((1, HEAD_DIM), lambda b, h, i: (0, 0)),
                  pl.BlockSpec(shift.shape, lambda b, h, i: (0, 0))],
        out_specs=pl.BlockSpec((tb, width), lambda b, h, i: (b * nt + i, h)),
        out_shape=jax.ShapeDtypeStruct((batch * seq, heads * HEAD_DIM), BF16),
        scratch_shapes=[pltpu.VMEM((group, HEAD_DIM, HEAD_DIM), F32)],
        compiler_params=_cparams(("arbitrary", "arbitrary", "arbitrary")),
        name="gated_deltanet",
    )(proj, proj, proj, proj, proj, proj, proj, conv_w, conv_w, conv_w, tok, tim, gnw, shift)


def _merge_kernel(osb_ref, ogdn_ref, wsb_ref, wgdn_ref, msb_ref, mgdn_ref, y_ref):
    a = _dot(osb_ref[...], wsb_ref[...])
    b = _dot(ogdn_ref[...], wgdn_ref[...])
    y = _sigmoid(msb_ref[...].astype(F32)) * a + _sigmoid(mgdn_ref[...].astype(F32)) * b
    y_ref[...] = y.astype(y_ref.dtype)


def _merge_call(o_sb, o_gdn, w_sb, w_gdn, proj, d):
    m, width = o_sb.shape
    tm = _tile(m, 1024)
    tn = _tile(d, 1024)
    nb = d // tn
    gate_base = 8 * d // tn
    return pl.pallas_call(
        _merge_kernel,
        grid=(nb, m // tm),
        in_specs=[pl.BlockSpec((tm, width), lambda j, i: (i, 0)),
                  pl.BlockSpec((tm, width), lambda j, i: (i, 0)),
                  pl.BlockSpec((width, tn), lambda j, i: (0, j)),
                  pl.BlockSpec((width, tn), lambda j, i: (0, j)),
                  pl.BlockSpec((tm, tn), lambda j, i: (i, gate_base + j)),
                  pl.BlockSpec((tm, tn), lambda j, i: (i, gate_base + nb + j))],
        out_specs=pl.BlockSpec((tm, tn), lambda j, i: (i, j)),
        out_shape=jax.ShapeDtypeStruct((m, d), BF16),
        compiler_params=_cparams(("arbitrary", "arbitrary")),
        name="gated_merge",
    )(o_sb, o_gdn, w_sb, w_gdn, proj, proj)


def _out_kernel(y_ref, w_ref, x_ref, gate_ref, fw_ref, o_ref):
    r = x_ref[...] + gate_ref[0] * _dot(y_ref[...], w_ref[...])
    ms = jnp.mean(r * r, axis=-1, keepdims=True)
    o_ref[...] = r * lax.rsqrt(ms + NORM_EPS) * fw_ref[...]


def _out_call(y, w_out, x2, gate3, fw, seq):
    m, d = x2.shape
    tm = _tile(seq, 512)
    per_b = seq // tm
    return pl.pallas_call(
        _out_kernel,
        grid=(m // tm,),
        in_specs=[pl.BlockSpec((tm, d), lambda i: (i, 0)),
                  pl.BlockSpec((d, d), lambda i: (0, 0)),
                  pl.BlockSpec((tm, d), lambda i: (i, 0)),
                  pl.BlockSpec((1, 1, d), lambda i: (i // per_b, 0, 0)),
                  pl.BlockSpec((1, d), lambda i: (0, 0))],
        out_specs=pl.BlockSpec((tm, d), lambda i: (i, 0)),
        out_shape=jax.ShapeDtypeStruct((m, d), F32),
        compiler_params=_cparams(("arbitrary",)),
        name="out_proj",
    )(y, w_out, x2, gate3, fw)


def _pad_lanes(a):
    return jnp.pad(a, ((0, 0), (0, HEAD_DIM - a.shape[1])))


def _layer(x2, c_pad, w_ada, b_ada, norm_w, w_in, conv_w, a_log, dt_bias, gdn_norm_w,
           w_proj_sb, w_proj_gdn, w_out, final_w, batch, seq):
    m, d = x2.shape
    heads = d // HEAD_DIM

    mod = _mod_call(c_pad, w_ada, b_ada.reshape(1, -1))[:batch]
    shift3 = mod[:, None, 0:d]
    scale3 = mod[:, None, d:2 * d]
    gate3 = mod[:, None, 2 * d:3 * d]
    nw = norm_w.reshape(1, d)

    proj = _inproj_call(x2, nw, scale3, shift3, w_in[:, :8 * d].astype(BF16),
                        w_in[:, 8 * d + 2 * heads:].astype(BF16), seq)

    tg = min(seq, 512)
    tpos = jnp.arange(tg)
    same_chunk = (tpos[:, None] // GDN_CHUNK) == (tpos[None, :] // GDN_CHUNK)
    tri = (same_chunk & (tpos[:, None] >= tpos[None, :])).astype(BF16)
    zero_h = jnp.zeros((heads,), F32)
    alog2 = jnp.concatenate([zero_h, a_log])
    dtb2 = jnp.concatenate([zero_h, dt_bias])
    tok, tim = _gates_call(
        x2, nw, scale3, shift3, w_in, 8 * d,
        _pad_lanes(alog2[None, :]), _pad_lanes(dtb2[None, :]), alog2[:, None], dtb2[:, None],
        tri, tri.T, seq, heads)

    kpos = jnp.arange(_tile(seq // 2, SB_KEY_BLOCK))
    upper = (kpos[:, None] > kpos[None, :]).astype(BF16)
    tq = min(SB_QUERY_BLOCK, upper.shape[0])
    o_sb = _sb_call(proj, upper, batch, seq, heads, tq, min(SB_SUBS, seq // tq))

    o_gdn = _gdn_call(proj, conv_w, tok, tim, gdn_norm_w.reshape(1, HEAD_DIM), batch, seq, heads,
                      _tile(seq, GDN_BLOCK), _tile(heads, GDN_HEAD_GROUP))

    y = _merge_call(o_sb, o_gdn, w_proj_sb.astype(BF16), w_proj_gdn.astype(BF16), proj, d)
    return _out_call(y, w_out.astype(BF16), x2, gate3, final_w, seq)


def kernel(x, c, w_ada, b_ada, norm_w, w_in, gdn_conv_w, gdn_a_log, gdn_dt_bias, gdn_norm_w,
           w_proj_sb, w_proj_gdn, w_out, final_norm_w):
    batch, seq, d = x.shape
    depth = w_ada.shape[0]
    assert depth == 1, "the final rmsnorm is fused into the single layer's output kernel"
    x2 = x.reshape(batch * seq, d)
    c_pad = jnp.pad(c, ((0, 8 - batch), (0, 0)))
    layer0 = lambda a: a.reshape(a.shape[1:])
    out = _layer(x2, c_pad, *(layer0(a) for a in (
        w_ada, b_ada, norm_w, w_in, gdn_conv_w, gdn_a_log, gdn_dt_bias, gdn_norm_w, w_proj_sb,
        w_proj_gdn, w_out)), final_norm_w.reshape(1, d), batch, seq)
    return out.reshape(batch, seq, d)
```
